```python
import math
import jax
import jax.numpy as jnp
from jax import lax
import numpy as np

D_MODEL = 4096
BATCH = 4
SEQ = 4096
DEPTH = 4

GRID_W = 64
CTX_LEN = 256
HEAD_DIM = 128
EPS = 1e-6
HY_W = D_MODEL // 2
HY_CONV = 3
HY_EMB = 33
HY_FF = 64
HY_TARGET = 1e-2
HY_FAST_PCT = 0.3
HY_SLOW_PCT = 1.5
ATT_HEADS = D_MODEL // 2 // HEAD_DIM
ATT_KV_HEADS = ATT_HEADS // 4
ATT_W = ATT_HEADS * HEAD_DIM
ATT_KV_W = ATT_KV_HEADS * HEAD_DIM
Q_BLOCK = 128
ROPE_THETA = 10000.0
LRU_W = D_MODEL // 2
LRU_BLOCKS = 16
LRU_BS = LRU_W // LRU_BLOCKS
LRU_CONV = 4
LRU_C = 8.0
NA_HEADS = D_MODEL // 2 // HEAD_DIM
NA_W = NA_HEADS * HEAD_DIM
NA_WIN_H = 8
NA_WIN_W = 16

EVEN_SPLITS = (3 * HY_W, HY_W, ATT_W, ATT_KV_W, ATT_KV_W, ATT_W)
ODD_SPLITS = (LRU_W, LRU_W, NA_W, NA_W, NA_W, NA_W)
EVEN_IN = sum(EVEN_SPLITS)
ODD_IN = sum(ODD_SPLITS)
EVEN_OUT = HY_W + ATT_W
ODD_OUT = LRU_W + NA_W

kernel_name = "hybrid_hyena_gqa_rglru_natten_flow_block"


def rmsnorm(x, g):
    x32 = x.astype(jnp.float32)
    y = x32 * lax.rsqrt(jnp.mean(x32 * x32, axis=-1, keepdims=True) + EPS)
    return (y * g.astype(jnp.float32)).astype(x.dtype)


def modulate(x, g, shift, scale):
    return rmsnorm(x, g) * (1 + scale) + shift


def split_cols(z, sizes):
    idx = [int(v) for v in np.cumsum(sizes)[:-1]]
    return jnp.split(z, idx, axis=-1)


def split_heads(t, n_heads):
    return t.reshape(t.shape[0], t.shape[1], n_heads, HEAD_DIM)


def dwconv(x, w, b, pad_lo, pad_hi):
    y = lax.conv_general_dilated(x, w[:, None, :].astype(x.dtype), window_strides=(1,),
                                 padding=[(pad_lo, pad_hi)],
                                 dimension_numbers=('NWC', 'WIO', 'NWC'),
                                 feature_group_count=x.shape[-1])
    return y + b


def rope_tables(L):
    pos = jnp.arange(L, dtype=jnp.int32)
    row = (pos // GRID_W).astype(jnp.float32)
    col = (pos % GRID_W).astype(jnp.float32)
    n = HEAD_DIM // 4
    inv = ROPE_THETA ** (-jnp.arange(n, dtype=jnp.float32) / n)
    ang = jnp.concatenate([row[:, None] * inv, col[:, None] * inv], axis=-1)
    return jnp.cos(ang), jnp.sin(ang)


def apply_rope(x, cos, sin):
    B, L, H, Dh = x.shape
    n = Dh // 4
    xr = x.astype(jnp.float32).reshape(B, L, H, 2, 2, n)
    c = cos.reshape(L, 1, 2, n)
    s = sin.reshape(L, 1, 2, n)
    x1 = xr[..., 0, :]
    x2 = xr[..., 1, :]
    out = jnp.stack([x1 * c - x2 * s, x1 * s + x2 * c], axis=-2)
    return out.reshape(B, L, H, Dh).astype(x.dtype)


def gqa_attend(q, k, v):
    B, Lq, H, Dh = q.shape
    kvh = k.shape[2]
    qg = q.reshape(B, Lq, kvh, H // kvh, Dh)
    s = jnp.einsum('bqkgd,btkd->bkgqt', qg, k, preferred_element_type=jnp.float32) * (Dh ** -0.5)
    p = jax.nn.softmax(s, axis=-1).astype(v.dtype)
    o = jnp.einsum('bkgqt,btkd->bqkgd', p, v)
    return o.reshape(B, Lq, H * Dh)


def blocked_attention(q, k, v):
    B, S, H, Dh = q.shape
    nblk = S // Q_BLOCK
    qb = q.reshape(B, nblk, Q_BLOCK, H, Dh).swapaxes(0, 1)
    o = lax.map(lambda qi: gqa_attend(qi, k, v), qb)
    return o.swapaxes(0, 1).reshape(B, S, H * Dh)


def hyena_filter(L, w1, b1, freq, w2, b2, w3):
    f32 = jnp.float32
    t = jnp.linspace(0.0, 1.0, L, dtype=f32)[:, None]
    bands = (HY_EMB - 1) // 2
    w = 2.0 * math.pi * jnp.arange(L, dtype=f32)[:, None] / L
    fr = jnp.linspace(1e-4, bands - 1, bands, dtype=f32)[None, :]
    feats = jnp.concatenate([t, jnp.cos(fr * w), -jnp.sin(fr * w)], axis=-1)
    om = freq.astype(f32)
    hdn = jnp.sin(om * (feats @ w1.astype(f32) + b1.astype(f32)))
    hdn = jnp.sin(om * (hdn @ w2.astype(f32) + b2.astype(f32)))
    h = hdn @ w3.astype(f32)
    deltas = jnp.linspace(abs(math.log(HY_TARGET) / HY_SLOW_PCT),
                          abs(math.log(HY_TARGET) / HY_FAST_PCT), HY_W, dtype=f32)
    decay = jnp.exp(-t * deltas[None, :])
    h_fwd = h[:, :HY_W] * decay
    h_bwd = h[:, HY_W:] * decay
    kern = jnp.concatenate([h_fwd, jnp.zeros((1, HY_W), f32), jnp.flip(h_bwd[1:], axis=0)], axis=0)
    return kern / jnp.sum(jnp.abs(kern), axis=0, keepdims=True)


def hyena_branch(u, conv_w, conv_b, w1, b1, freq, w2, b2, w3, skip):
    L = u.shape[1]
    u = dwconv(u, conv_w, conv_b, HY_CONV // 2, HY_CONV // 2)
    x0, x1, v = jnp.split(u, 3, axis=-1)
    kern = hyena_filter(L, w1, b1, freq, w2, b2, w3)
    s = (x1 * v).astype(jnp.float32)
    y = jnp.fft.irfft(jnp.fft.rfft(s, n=2 * L, axis=1) * jnp.fft.rfft(kern, axis=0)[None],
                      n=2 * L, axis=1)[:, :L]
    y = y + s * skip.astype(jnp.float32)
    return (x0.astype(jnp.float32) * y).astype(u.dtype)


def blockdiag(x, w, b):
    B, L, C = x.shape
    y = jnp.einsum('blnd,nde->blne', x.reshape(B, L, LRU_BLOCKS, LRU_BS), w).reshape(B, L, C)
    return y + b


def lru_coeffs(xr, wa, ba, wx, bx, lam):
    r = jax.nn.sigmoid(blockdiag(xr, wa, ba).astype(jnp.float32))
    i = jax.nn.sigmoid(blockdiag(xr, wx, bx).astype(jnp.float32))
    log_a = -LRU_C * r * jax.nn.softplus(-lam.astype(jnp.float32))
    a = jnp.exp(log_a)
    b = jnp.sqrt(-jnp.expm1(2.0 * log_a)) * (i * xr.astype(jnp.float32))
    return a, b


def linear_scan(a, b, h0):
    b = b.at[:, 0].add(a[:, 0] * h0)

    def comb(c1, c2):
        return c1[0] * c2[0], c2[0] * c1[1] + c2[1]

    _, h = lax.associative_scan(comb, (a, b), axis=1)
    return h


def rglru_branch(u_lat, u_ctx, conv_w, conv_b, wa, ba, wx, bx, lam, need_ctx):
    xl = dwconv(u_lat, conv_w, conv_b, 2, 1)
    xc = dwconv(u_ctx, conv_w, conv_b, 2, 1)
    y_lat = None
    y_ctx = None
    for d in range(2):
        al, bl = lru_coeffs(xl, wa[d], ba[d], wx[d], bx[d], lam[d])
        ac, bc = lru_coeffs(xc, wa[d], ba[d], wx[d], bx[d], lam[d])
        if d == 1:
            al, bl, ac, bc = (jnp.flip(t, axis=1) for t in (al, bl, ac, bc))
        hc = linear_scan(ac, bc, jnp.zeros_like(bc[:, 0]))
        hl = linear_scan(al, bl, hc[:, -1])
        if d == 1:
            hl = jnp.flip(hl, axis=1)
            hc = jnp.flip(hc, axis=1)
        y_lat = hl if y_lat is None else y_lat + hl
        y_ctx = hc if y_ctx is None else y_ctx + hc
    y_lat = y_lat.astype(u_lat.dtype)
    y_ctx = y_ctx.astype(u_ctx.dtype) if need_ctx else None
    return y_lat, y_ctx


def na_branch(q, k, v, kc, vc, rpb):
    B, S, H, Dh = q.shape
    rows = S // GRID_W
    wh = min(NA_WIN_H, rows)
    r = jnp.arange(rows, dtype=jnp.int32)
    cc = jnp.arange(GRID_W, dtype=jnp.int32)
    r0 = jnp.clip(r - wh // 2, 0, rows - wh)
    row_idx = r0[:, None] + jnp.arange(wh, dtype=jnp.int32)[None, :]
    row_off = row_idx - r[:, None] + (NA_WIN_H - 1)
    c0 = jnp.clip(cc - NA_WIN_W // 2, 0, GRID_W - NA_WIN_W)
    col_idx = c0[:, None] + jnp.arange(NA_WIN_W, dtype=jnp.int32)[None, :]
    col_off = col_idx - cc[:, None] + (NA_WIN_W - 1)
    rpb_cols = rpb[:, :, col_off]
    kg = k.reshape(B, rows, GRID_W, H, Dh)
    vg = v.reshape(B, rows, GRID_W, H, Dh)
    qg = q.reshape(B, rows, GRID_W, H, Dh).swapaxes(0, 1)
    scale = Dh ** -0.5
    n_win = wh * NA_WIN_W

    def row_block(args):
        q_r, ridx, roff = args
        k_win = jnp.take(jnp.take(kg, ridx, axis=1), col_idx, axis=2)
        v_win = jnp.take(jnp.take(vg, ridx, axis=1), col_idx, axis=2)
        bias = rpb_cols[:, roff].swapaxes(1, 2).astype(jnp.float32)
        s_win = jnp.einsum('bqhd,bwqvhd->bhqwv', q_r, k_win,
                           preferred_element_type=jnp.float32) * scale + bias[None]
        s_ctx = jnp.einsum('bqhd,bchd->bhqc', q_r, kc, preferred_element_type=jnp.float32) * scale
        s = jnp.concatenate([s_win.reshape(B, H, GRID_W, n_win), s_ctx], axis=-1)
        p = jax.nn.softmax(s, axis=-1)
        p_win = p[..., :n_win].reshape(B, H, GRID_W, wh, NA_WIN_W).astype(v.dtype)
        p_ctx = p[..., n_win:].astype(v.dtype)
        return (jnp.einsum('bhqwv,bwqvhd->bqhd', p_win, v_win)
                + jnp.einsum('bhqc,bchd->bqhd', p_ctx, vc))

    o = lax.map(row_block, (qg, row_idx, row_off))
    return o.swapaxes(0, 1).reshape(B, S, H * Dh)


def even_layer(h, hc, w_in, w_out, conv_w, conv_b, w1, b1, freq, w2, b2, w3, skip, q_g, k_g, need_ctx):
    hy_u, hy_g, q, k, v, att_g = split_cols(h @ w_in, EVEN_SPLITS)
    hy_uc, hy_gc, qc, kc, vc, att_gc = split_cols(hc @ w_in, EVEN_SPLITS)
    S = h.shape[1]
    y_hy = hyena_branch(hy_u, conv_w, conv_b, w1, b1, freq, w2, b2, w3, skip)
    cos, sin = rope_tables(S)
    q = apply_rope(rmsnorm(split_heads(q, ATT_HEADS), q_g), cos, sin)
    k = apply_rope(rmsnorm(split_heads(k, ATT_KV_HEADS), k_g), cos, sin)
    v = split_heads(v, ATT_KV_HEADS)
    kc = rmsnorm(split_heads(kc, ATT_KV_HEADS), k_g)
    vc = split_heads(vc, ATT_KV_HEADS)
    y_att = blocked_attention(q, jnp.concatenate([k, kc], axis=1), jnp.concatenate([v, vc], axis=1))
    out = jnp.concatenate([y_hy * jax.nn.silu(hy_g), y_att * jax.nn.silu(att_g)], axis=-1) @ w_out
    out_c = None
    if need_ctx:
        y_hyc = hyena_branch(hy_uc, conv_w, conv_b, w1, b1, freq, w2, b2, w3, skip)
        qc = rmsnorm(split_heads(qc, ATT_HEADS), q_g)
        y_attc = gqa_attend(qc, kc, vc)
        out_c = jnp.concatenate([y_hyc * jax.nn.silu(hy_gc), y_attc * jax.nn.silu(att_gc)], axis=-1) @ w_out
    return out, out_c


def odd_layer(h, hc, w_in, w_out, conv_w, conv_b, wa, ba, wx, bx, lam, rpb, need_ctx):
    lru_u, lru_g, q, k, v, na_g = split_cols(h @ w_in, ODD_SPLITS)
    lru_uc, lru_gc, qc, kc, vc, na_gc = split_cols(hc @ w_in, ODD_SPLITS)
    y_lru, y_lruc = rglru_branch(lru_u, lru_uc, conv_w, conv_b, wa, ba, wx, bx, lam, need_ctx)
    kc = split_heads(kc, NA_HEADS)
    vc = split_heads(vc, NA_HEADS)
    y_na = na_branch(split_heads(q, NA_HEADS), split_heads(k, NA_HEADS), split_heads(v, NA_HEADS), kc, vc, rpb)
    out = jnp.concatenate([y_lru * jax.nn.silu(lru_g), y_na * jax.nn.silu(na_g)], axis=-1) @ w_out
    out_c = None
    if need_ctx:
        y_nac = gqa_attend(split_heads(qc, NA_HEADS), kc, vc)
        out_c = jnp.concatenate([y_lruc * jax.nn.silu(lru_gc), y_nac * jax.nn.silu(na_gc)], axis=-1) @ w_out
    return out, out_c


def setup_inputs(seed: int = 0) -> dict:
    key = jax.random.key(seed)
    ks = jax.random.split(key, 32)
    f32 = jnp.float32
    n_ev = (DEPTH + 1) // 2
    n_od = DEPTH // 2

    def nrm(k, shape, fan):
        return jax.random.normal(k, shape, f32) * (fan ** -0.5)

    def small(k, shape, s=0.02):
        return jax.random.normal(k, shape, f32) * s

    a_c = jax.random.uniform(ks[30], (n_od, 2, LRU_W), f32, minval=0.9, maxval=0.999)
    a_base = a_c ** (1.0 / LRU_C)
    lru_lambda = jnp.log(a_base) - jnp.log1p(-a_base)
    return {
        "x": jax.random.normal(ks[0], (BATCH, SEQ, D_MODEL), f32),
        "c": jax.random.normal(ks[1], (BATCH, D_MODEL), f32),
        "ctx": jax.random.normal(ks[2], (BATCH, CTX_LEN, D_MODEL), f32),
        "c_ctx": jax.random.normal(ks[3], (D_MODEL,), f32),
        "w_mod": nrm(ks[4], (DEPTH, D_MODEL, 3 * D_MODEL), D_MODEL) * 0.5,
        "b_mod": small(ks[5], (DEPTH, 3 * D_MODEL)),
        "norm_g": 1.0 + small(ks[6], (DEPTH, D_MODEL), 0.05),
        "final_g": 1.0 + small(ks[7], (D_MODEL,), 0.05),
        "ev_w_in": nrm(ks[8], (n_ev, D_MODEL, EVEN_IN), D_MODEL),
        "ev_w_out": nrm(ks[9], (n_ev, EVEN_OUT, D_MODEL), EVEN_OUT),
        "hy_conv_w": nrm(ks[10], (n_ev, HY_CONV, 3 * HY_W), HY_CONV),
        "hy_conv_b": small(ks[11], (n_ev, 3 * HY_W)),
        "hy_w1": nrm(ks[12], (n_ev, HY_EMB, HY_FF), HY_EMB),
        "hy_b1": small(ks[13], (n_ev, HY_FF)),
        "hy_freq": 1.0 + small(ks[14], (n_ev, HY_FF), 0.05),
        "hy_w2": nrm(ks[15], (n_ev, HY_FF, HY_FF), HY_FF),
        "hy_b2": small(ks[16], (n_ev, HY_FF)),
        "hy_w3": nrm(ks[17], (n_ev, HY_FF, 2 * HY_W), HY_FF),
        "hy_skip": jax.random.normal(ks[18], (n_ev, HY_W), f32),
        "att_q_g": 1.0 + small(ks[19], (n_ev, HEAD_DIM), 0.05),
        "att_k_g": 1.0 + small(ks[20], (n_ev, HEAD_DIM), 0.05),
        "od_w_in": nrm(ks[21], (n_od, D_MODEL, ODD_IN), D_MODEL),
        "od_w_out": nrm(ks[22], (n_od, ODD_OUT, D_MODEL), ODD_OUT),
        "lru_conv_w": nrm(ks[23], (n_od, LRU_CONV, LRU_W), LRU_CONV),
        "lru_conv_b": small(ks[24], (n_od, LRU_W)),
        "lru_wa": nrm(ks[25], (n_od, 2, LRU_BLOCKS, LRU_BS, LRU_BS), LRU_BS),
        "lru_ba": small(ks[26], (n_od, 2, LRU_W)),
        "lru_wx": nrm(ks[27], (n_od, 2, LRU_BLOCKS, LRU_BS, LRU_BS), LRU_BS),
        "lru_bx": small(ks[28], (n_od, 2, LRU_W)),
        "lru_lambda": lru_lambda,
        "na_rpb": small(ks[29], (n_od, NA_HEADS, 2 * NA_WIN_H - 1, 2 * NA_WIN_W - 1), 0.1),
    }


def reference(x, c, ctx, c_ctx, w_mod, b_mod, norm_g, final_g,
              ev_w_in, ev_w_out, hy_conv_w, hy_conv_b, hy_w1, hy_b1, hy_freq, hy_w2, hy_b2, hy_w3,
              hy_skip, att_q_g, att_k_g,
              od_w_in, od_w_out, lru_conv_w, lru_conv_b, lru_wa, lru_ba, lru_wx, lru_bx,
              lru_lambda, na_rpb):
    sc = jax.nn.silu(c)
    sc_ctx = jax.nn.silu(c_ctx)
    for i in range(DEPTH):
        need_ctx = i < DEPTH - 1
        mod = sc @ w_mod[i] + b_mod[i]
        shift, scale, gate = jnp.split(mod[:, None, :], 3, axis=-1)
        mod_c = sc_ctx @ w_mod[i] + b_mod[i]
        shift_c, scale_c, gate_c = jnp.split(mod_c[None, None, :], 3, axis=-1)
        h = modulate(x, norm_g[i], shift, scale)
        hc = modulate(ctx, norm_g[i], shift_c, scale_c)
        j = i // 2
        if i % 2 == 0:
            out, out_c = even_layer(h, hc, ev_w_in[j], ev_w_out[j], hy_conv_w[j], hy_conv_b[j],
                                    hy_w1[j], hy_b1[j], hy_freq[j], hy_w2[j], hy_b2[j], hy_w3[j],
                                    hy_skip[j], att_q_g[j], att_k_g[j], need_ctx)
        else:
            out, out_c = odd_layer(h, hc, od_w_in[j], od_w_out[j], lru_conv_w[j], lru_conv_b[j],
                                   lru_wa[j], lru_ba[j], lru_wx[j], lru_bx[j], lru_lambda[j],
                                   na_rpb[j], need_ctx)
        x = x + gate * out
        if need_ctx:
            ctx = ctx + gate_c * out_c
    return rmsnorm(x, final_g)
```

```python
import functools
import math

import numpy as np
import jax
import jax.numpy as jnp
from jax import lax
from jax.experimental import pallas as pl
from jax.experimental.pallas import tpu as pltpu

F32 = jnp.float32
BF16 = jnp.bfloat16

D_MODEL = 4096
HEAD_DIM = 128
GRID_W = 64
EPS = 1e-6
HY_W = D_MODEL // 2
HY_EMB = 33
HY_FF = 64
HY_TARGET = 1e-2
HY_FAST_PCT = 0.3
HY_SLOW_PCT = 1.5
ATT_HEADS = D_MODEL // 2 // HEAD_DIM
ATT_KV_HEADS = ATT_HEADS // 4
ATT_GROUP = ATT_HEADS // ATT_KV_HEADS
ROPE_THETA = 10000.0
LRU_W = D_MODEL // 2
LRU_BS = 128
LRU_C = 8.0
NA_HEADS = D_MODEL // 2 // HEAD_DIM
NA_WIN_H = 8
NA_WIN_W = 16
NA_TILE_ROWS = 8
NA_KEY_ROWS = 16
NEG_BIG = -1e30

LANES = 128
SUBLANES = 8
FFT_N2 = 128
VMEM_LIMIT_MB = 56

EV_X0, EV_X1, EV_V, EV_HG = 0, 16, 32, 48
EV_Q, EV_K, EV_VV, EV_AG = 64, 80, 84, 88
EVEN_IN = 13312
OD_U, OD_G, OD_Q, OD_K, OD_V, OD_NG = 0, 16, 32, 48, 64, 80
ODD_IN = 12288


def _cp(sem, vmem_mb=VMEM_LIMIT_MB):
    return pltpu.CompilerParams(dimension_semantics=sem, vmem_limit_bytes=vmem_mb * 1024 * 1024)


def _silu(x):
    return x * jax.nn.sigmoid(x)


def _mod_body(c_ref, w_ref, b_ref, o_ref):
    sc = _silu(c_ref[...])
    o_ref[0] = jnp.dot(sc.astype(BF16), w_ref[0].astype(BF16), preferred_element_type=F32) + b_ref[0]


def _mod_call(c8, w_mod, b_mod):
    depth, d, n3 = w_mod.shape
    tn = 512
    return pl.pallas_call(
        _mod_body,
        grid=(depth, n3 // tn),
        in_specs=[pl.BlockSpec((8, d), lambda l, j: (0, 0)),
                  pl.BlockSpec((1, d, tn), lambda l, j: (l, 0, j)),
                  pl.BlockSpec((1, 1, tn), lambda l, j: (l, 0, j))],
        out_specs=pl.BlockSpec((1, 8, tn), lambda l, j: (l, 0, j)),
        out_shape=jax.ShapeDtypeStruct((depth, 8, n3), F32),
        compiler_params=_cp(("parallel", "parallel")),
        name="mod_vectors",
    )(c8, w_mod, b_mod.reshape(depth, 1, n3))


def _normmod_body(x_ref, g_ref, sh_ref, sc_ref, o_ref):
    x = x_ref[...]
    ms = jnp.mean(x * x, axis=-1, keepdims=True)
    y = x * lax.rsqrt(ms + EPS) * g_ref[...]
    o_ref[...] = (y * (1.0 + sc_ref[0]) + sh_ref[0]).astype(BF16)


def _mod_row_fn(tr, seq_len, fixed_row):
    if fixed_row is not None:
        return lambda i: fixed_row
    per = seq_len // tr
    return lambda i: i // per


def _normmod_call(x2, mod3, g, seq_len, fixed_row):
    m, d = x2.shape
    tr = 256
    row = _mod_row_fn(tr, seq_len, fixed_row)
    return pl.pallas_call(
        _normmod_body,
        grid=(m // tr,),
        in_specs=[pl.BlockSpec((tr, d), lambda i: (i, 0)),
                  pl.BlockSpec((1, d), lambda i: (0, 0)),
                  pl.BlockSpec((1, 1, d), lambda i: (row(i), 0, 0)),
                  pl.BlockSpec((1, 1, d), lambda i: (row(i), 0, 1))],
        out_specs=pl.BlockSpec((tr, d), lambda i: (i, 0)),
        out_shape=jax.ShapeDtypeStruct((m, d), BF16),
        compiler_params=_cp(("parallel",)),
        name="norm_modulate",
    )(x2, g.reshape(1, d), mod3, mod3)


def _mm_in_body(a_ref, w_ref, o_ref):
    o_ref[...] = jnp.dot(a_ref[...], w_ref[...], preferred_element_type=F32)


def _mm_in_call(a, w):
    m, k = a.shape
    n = w.shape[1]
    tm = min(1024, m)
    tn = 512
    return pl.pallas_call(
        _mm_in_body,
        grid=(m // tm, n // tn),
        in_specs=[pl.BlockSpec((tm, k), lambda i, j: (i, 0)),
                  pl.BlockSpec((k, tn), lambda i, j: (0, j))],
        out_specs=pl.BlockSpec((tm, tn), lambda i, j: (i, j)),
        out_shape=jax.ShapeDtypeStruct((m, n), F32),
        compiler_params=_cp(("parallel", "parallel")),
        name="proj_in",
    )(a, w)


def _mm_out_body(a0_ref, a1_ref, w0_ref, w1_ref, r_ref, gt_ref, o_ref):
    acc = jnp.dot(a0_ref[...], w0_ref[...], preferred_element_type=F32)
    acc = acc + jnp.dot(a1_ref[...], w1_ref[...], preferred_element_type=F32)
    o_ref[...] = r_ref[...] + gt_ref[0] * acc


def _mm_out_call(a0, a1, w, resid, mod3, seq_len, fixed_row):
    m, kh = a0.shape
    n = w.shape[1]
    tm = min(1024, m, seq_len if fixed_row is None else m)
    tn = 512
    row = _mod_row_fn(tm, seq_len, fixed_row)
    gate_blk = 2 * (n // tn)
    return pl.pallas_call(
        _mm_out_body,
        grid=(m // tm, n // tn),
        in_specs=[pl.BlockSpec((tm, kh), lambda i, j: (i, 0)),
                  pl.BlockSpec((tm, kh), lambda i, j: (i, 0)),
                  pl.BlockSpec((kh, tn), lambda i, j: (0, j)),
                  pl.BlockSpec((kh, tn), lambda i, j: (1, j)),
                  pl.BlockSpec((tm, tn), lambda i, j: (i, j)),
                  pl.BlockSpec((1, 1, tn), lambda i, j: (row(i), 0, gate_blk + j))],
        out_specs=pl.BlockSpec((tm, tn), lambda i, j: (i, j)),
        out_shape=jax.ShapeDtypeStruct((m, n), F32),
        compiler_params=_cp(("parallel", "parallel")),
        name="proj_out",
    )(a0, a1, w, w, resid, mod3)


def _final_body(x_ref, g_ref, o_ref):
    x = x_ref[...]
    ms = jnp.mean(x * x, axis=-1, keepdims=True)
    o_ref[...] = x * lax.rsqrt(ms + EPS) * g_ref[...]


def _final_call(x2, g):
    m, d = x2.shape
    tr = 256
    return pl.pallas_call(
        _final_body,
        grid=(m // tr,),
        in_specs=[pl.BlockSpec((tr, d), lambda i: (i, 0)),
                  pl.BlockSpec((1, d), lambda i: (0, 0))],
        out_specs=pl.BlockSpec((tr, d), lambda i: (i, 0)),
        out_shape=jax.ShapeDtypeStruct((m, d), F32),
        compiler_params=_cp(("parallel",)),
        name="final_norm",
    )(x2, g.reshape(1, d))


def _shift_rows(u, rows, k, length):
    if k == 0:
        return u
    if k > 0:
        return jnp.where(rows >= k, pltpu.roll(u, k, 0), 0.0)
    return jnp.where(rows < length + k, pltpu.roll(u, length + k, 0), 0.0)


def _hyprep_body(x0_ref, x1_ref, v_ref, g_ref, w0_ref, w1_ref, w2_ref, b0_ref, b1_ref, b2_ref,
                 x0g_ref, s_ref, *, length):
    rows = lax.broadcasted_iota(jnp.int32, (length, LANES), 0)

    def conv(u_ref, w_ref, b_ref):
        u = u_ref[...]
        return (w_ref[0:1, :] * _shift_rows(u, rows, 1, length) + w_ref[1:2, :] * u
                + w_ref[2:3, :] * _shift_rows(u, rows, -1, length) + b_ref[...])

    x0 = conv(x0_ref, w0_ref, b0_ref)
    x1 = conv(x1_ref, w1_ref, b1_ref)
    v = conv(v_ref, w2_ref, b2_ref)
    x0g_ref[...] = x0 * _silu(g_ref[...])
    s_ref[...] = x1 * v


def _hyprep_call(z, conv_w, conv_b, length):
    m = z.shape[0]
    nseq = m // length
    nc = HY_W // LANES
    cb = conv_b.reshape(1, -1)
    zspec = lambda off: pl.BlockSpec((length, LANES), lambda b, c: (b, off + c))
    wspec = lambda off: pl.BlockSpec((3, LANES), lambda b, c: (0, off + c))
    bspec = lambda off: pl.BlockSpec((1, LANES), lambda b, c: (0, off + c))
    ospec = pl.BlockSpec((length, LANES), lambda b, c: (b, c))
    return pl.pallas_call(
        functools.partial(_hyprep_body, length=length),
        grid=(nseq, nc),
        in_specs=[zspec(EV_X0), zspec(EV_X1), zspec(EV_V), zspec(EV_HG),
                  wspec(EV_X0), wspec(EV_X1), wspec(EV_V),
                  bspec(EV_X0), bspec(EV_X1), bspec(EV_V)],
        out_specs=[ospec, ospec],
        out_shape=[jax.ShapeDtypeStruct((m, HY_W), F32), jax.ShapeDtypeStruct((m, HY_W), F32)],
        compiler_params=_cp(("parallel", "parallel")),
        name="hyena_prep",
    )(z, z, z, z, conv_w, conv_w, conv_w, cb, cb, cb)


def _filter_body(f_ref, w1_ref, b1_ref, fq_ref, w2_ref, b2_ref, w3_ref, dl_ref, k_ref, as_ref,
                 *, length, tr):
    i = pl.program_id(0)
    hp = lax.Precision.HIGHEST
    f = f_ref[...]
    fq = fq_ref[...]
    h1 = jnp.sin(fq * (jnp.dot(f, w1_ref[...], precision=hp, preferred_element_type=F32) + b1_ref[...]))
    h2 = jnp.sin(fq * (jnp.dot(h1, w2_ref[...], precision=hp, preferred_element_type=F32) + b2_ref[...]))
    h = jnp.dot(h2, w3_ref[...], precision=hp, preferred_element_type=F32)
    t = f[:, 0:1]
    dec = jnp.exp(-t * dl_ref[...])
    row = i * tr + lax.broadcasted_iota(jnp.int32, (tr, 1), 0)
    kv = jnp.where(row == length, 0.0, h * dec)
    k_ref[...] = kv

    @pl.when(i == 0)
    def _():
        as_ref[...] = jnp.zeros_like(as_ref)

    as_ref[...] += jnp.sum(jnp.abs(kv), axis=0, keepdims=True)


def _filter_call(feats2, w1p, b1p, fqp, w2p, b2p, w3p, deltas, length):
    n = 2 * length
    tr = min(512, length)
    per_half = length // tr
    full = lambda shape: pl.BlockSpec(shape, lambda i: (0, 0))
    return pl.pallas_call(
        functools.partial(_filter_body, length=length, tr=tr),
        grid=(n // tr,),
        in_specs=[pl.BlockSpec((tr, LANES), lambda i: (i, 0)),
                  full((LANES, LANES)), full((1, LANES)), full((1, LANES)),
                  full((LANES, LANES)), full((1, LANES)),
                  pl.BlockSpec((LANES, HY_W), lambda i: (0, i // per_half)),
                  full((1, HY_W))],
        out_specs=[pl.BlockSpec((tr, HY_W), lambda i: (i, 0)), full((1, HY_W))],
        out_shape=[jax.ShapeDtypeStruct((n, HY_W), F32), jax.ShapeDtypeStruct((1, HY_W), F32)],
        compiler_params=_cp(("arbitrary",)),
        name="hyena_filter",
    )(feats2, w1p, b1p, fqp, w2p, b2p, w3p, deltas)


def _w(n, big_n, sgn):
    n = n % big_n
    if (4 * n) % big_n == 0:
        q = (4 * n) // big_n
        wr, wi = [(1.0, 0.0), (0.0, 1.0), (-1.0, 0.0), (0.0, -1.0)][q]
        return (wr, sgn * wi)
    ang = 2.0 * math.pi * n / big_n
    return (math.cos(ang), sgn * math.sin(ang))


def _smul(a, c):
    if a is None or c == 0.0:
        return None
    if c == 1.0:
        return a
    if c == -1.0:
        return -a
    return a * c


def _sadd(a, b):
    if a is None:
        return b
    if b is None:
        return a
    return a + b


def _ssub(a, b):
    if b is None:
        return a
    if a is None:
        return -b
    return a - b


def _cmulc(x, w):
    xr, xi = x
    wr, wi = w
    return (_ssub(_smul(xr, wr), _smul(xi, wi)), _sadd(_smul(xr, wi), _smul(xi, wr)))


def _dft_list(xs, sgn):
    n = len(xs)
    if n == 1:
        return list(xs)
    ev = _dft_list(xs[0::2], sgn)
    od = _dft_list(xs[1::2], sgn)
    out = [None] * n
    for k in range(n // 2):
        t = _cmulc(od[k], _w(k, n, sgn))
        out[k] = (_sadd(ev[k][0], t[0]), _sadd(ev[k][1], t[1]))
        out[k + n // 2] = (_ssub(ev[k][0], t[0]), _ssub(ev[k][1], t[1]))
    return out


def _split_rc(n1):
    lg = int(round(math.log2(n1)))
    assert (1 << lg) == n1
    r = 1 << (lg // 2)
    return r, n1 // r


def _store_piece(a_ref, slot, r0, val):
    for ri in range(2):
        v = val[ri]
        if v is None:
            v = jnp.zeros((SUBLANES, LANES), F32)
        a_ref[slot, ri, pl.ds(r0, SUBLANES), :] = v


def _load_piece(a_ref, slot, r0):
    return (a_ref[slot, 0, pl.ds(r0, SUBLANES), :], a_ref[slot, 1, pl.ds(r0, SUBLANES), :])


def _lead_forward(load_in, a_ref, n1, in_half):
    r, c = _split_rc(n1)
    nrt = FFT_N2 // SUBLANES

    def pass_a(rt, carry):
        r0 = pl.multiple_of(rt * SUBLANES, SUBLANES)
        for n_c in range(c):
            xs = []
            for n_r in range(r):
                if in_half and n_r >= r // 2:
                    xs.append((None, None))
                else:
                    xs.append(load_in(c * n_r + n_c, r0))
            ys = _dft_list(xs, -1)
            for k_r in range(r):
                _store_piece(a_ref, c * k_r + n_c, r0, _cmulc(ys[k_r], _w(n_c * k_r, n1, -1)))
        return carry

    lax.fori_loop(0, nrt, pass_a, 0)

    def pass_b(rt, carry):
        r0 = pl.multiple_of(rt * SUBLANES, SUBLANES)
        for k_r in range(r):
            xs = [_load_piece(a_ref, c * k_r + n_c, r0) for n_c in range(c)]
            ys = _dft_list(xs, -1)
            for k_c in range(c):
                _store_piece(a_ref, c * k_r + k_c, r0, ys[k_c])
        return carry

    lax.fori_loop(0, nrt, pass_b, 0)


def _lead_inverse_half(a_ref, n1):
    r, c = _split_rc(n1)
    nrt = FFT_N2 // SUBLANES

    def pass_a(rt, carry):
        r0 = pl.multiple_of(rt * SUBLANES, SUBLANES)
        for k_r in range(r):
            xs = [_load_piece(a_ref, c * k_r + k_c, r0) for k_c in range(c)]
            ys = _dft_list(xs, 1)
            for n_c in range(c):
                _store_piece(a_ref, c * k_r + n_c, r0, _cmulc(ys[n_c], _w(n_c * k_r, n1, 1)))
        return carry

    lax.fori_loop(0, nrt, pass_a, 0)

    def pass_b(rt, carry):
        r0 = pl.multiple_of(rt * SUBLANES, SUBLANES)
        for n_c in range(c):
            xs = [_load_piece(a_ref, c * k_r + n_c, r0) for k_r in range(r)]
            ys = _dft_list(xs, 1)
            for n_r in range(r // 2):
                _store_piece(a_ref, c * n_r + n_c, r0, ys[n_r])
        return carry

    lax.fori_loop(0, nrt, pass_b, 0)


def _dot3(gh, gl, x):
    xh = x.astype(BF16)
    xl = (x - xh.astype(F32)).astype(BF16)
    return (jnp.dot(gh, xh, preferred_element_type=F32) + jnp.dot(gh, xl, preferred_element_type=F32)
            + jnp.dot(gl, xh, preferred_element_type=F32))


def _slab_twiddle(p, c, tar_ref, tai_ref, tbr_ref, tbi_ref):
    k_r = p // c
    k_c = p % c
    ar, ai = tar_ref[k_r], tai_ref[k_r]
    br, bi = tbr_ref[k_c], tbi_ref[k_c]
    return ar * br - ai * bi, ar * bi + ai * br


def _fft_consts(length):
    n = 2 * length
    n1 = n // FFT_N2
    r, c = _split_rc(n1)
    idx = np.arange(FFT_N2)
    ang = -2.0 * np.pi * np.outer(idx, idx) / FFT_N2
    fr, fi = np.cos(ang), np.sin(ang)
    gf = np.block([[fr, -fi], [fi, fr]])
    gi = np.block([[fr, fi], [-fi, fr]])

    def split(mat):
        m32 = jnp.asarray(mat, F32)
        hi = m32.astype(BF16)
        lo = (m32 - hi.astype(F32)).astype(BF16)
        return hi, lo

    def table(mult, count):
        a = -2.0 * np.pi * np.outer(np.arange(count) * mult, idx) / n
        tr = np.broadcast_to(np.cos(a)[:, :, None], (count, FFT_N2, LANES))
        ti = np.broadcast_to(np.sin(a)[:, :, None], (count, FFT_N2, LANES))
        return jnp.asarray(tr, F32), jnp.asarray(ti, F32)

    gfh, gfl = split(gf)
    gih, gil = split(gi)
    tar, tai = table(1, r)
    tbr, tbi = table(r, c)
    return dict(n1=n1, gfh=gfh, gfl=gfl, gih=gih, gil=gil, tar=tar, tai=tai, tbr=tbr, tbi=tbi)


def _kfft_body(k_ref, as_ref, gfh_ref, gfl_ref, tar_ref, tai_ref, tbr_ref, tbi_ref, o_ref, *, n1):
    a_ref = o_ref.at[0]
    _, c = _split_rc(n1)
    scale = (1.0 / as_ref[...]) * (1.0 / (n1 * FFT_N2))

    def load_in(slab, r0):
        return (k_ref[pl.ds(slab * FFT_N2 + r0, SUBLANES), :] * scale, None)

    _lead_forward(load_in, a_ref, n1, in_half=False)

    def stage(p, carry):
        tr, ti = _slab_twiddle(p, c, tar_ref, tai_ref, tbr_ref, tbi_ref)
        ar, ai = a_ref[p, 0], a_ref[p, 1]
        x = jnp.concatenate([ar * tr - ai * ti, ar * ti + ai * tr], axis=0)
        y = _dot3(gfh_ref[...], gfl_ref[...], x)
        a_ref[p, 0] = y[:FFT_N2]
        a_ref[p, 1] = y[FFT_N2:]
        return carry

    lax.fori_loop(0, n1, stage, 0)


def _kfft_call(kern, asum, fc):
    n1 = fc["n1"]
    nc = HY_W // LANES
    full2 = lambda shape: pl.BlockSpec(shape, lambda j: (0, 0))
    full3 = lambda shape: pl.BlockSpec(shape, lambda j: (0, 0, 0))
    return pl.pallas_call(
        functools.partial(_kfft_body, n1=n1),
        grid=(nc,),
        in_specs=[pl.BlockSpec((n1 * FFT_N2, LANES), lambda j: (0, j)),
                  pl.BlockSpec((1, LANES), lambda j: (0, j)),
                  full2(fc["gfh"].shape), full2(fc["gfl"].shape),
                  full3(fc["tar"].shape), full3(fc["tai"].shape),
                  full3(fc["tbr"].shape), full3(fc["tbi"].shape)],
        out_specs=pl.BlockSpec((1, n1, 2, FFT_N2, LANES), lambda j: (j, 0, 0, 0, 0)),
        out_shape=jax.ShapeDtypeStruct((nc, n1, 2, FFT_N2, LANES), F32),
        compiler_params=_cp(("parallel",)),
        name="hyena_filter_fft",
    )(kern, asum, fc["gfh"], fc["gfl"], fc["tar"], fc["tai"], fc["tbr"], fc["tbi"])


def _fftconv_body(s_ref, x0g_ref, kh_ref, skip_ref, gfh_ref, gfl_ref, gih_ref, gil_ref,
                  tar_ref, tai_ref, tbr_ref, tbi_ref, o_ref, a_ref, *, n1, length):
    _, c = _split_rc(n1)

    def load_in(slab, r0):
        return (s_ref[pl.ds(slab * FFT_N2 + r0, SUBLANES), :],
                s_ref[pl.ds(length + slab * FFT_N2 + r0, SUBLANES), :])

    _lead_forward(load_in, a_ref, n1, in_half=True)

    def stage(p, carry):
        tr, ti = _slab_twiddle(p, c, tar_ref, tai_ref, tbr_ref, tbi_ref)
        ar, ai = a_ref[p, 0], a_ref[p, 1]
        x = jnp.concatenate([ar * tr - ai * ti, ar * ti + ai * tr], axis=0)
        xf = _dot3(gfh_ref[...], gfl_ref[...], x)
        xr, xi = xf[:FFT_N2], xf[FFT_N2:]
        kr, ki = kh_ref[0, p, 0], kh_ref[0, p, 1]
        y = jnp.concatenate([xr * kr - xi * ki, xr * ki + xi * kr], axis=0)
        bt = _dot3(gih_ref[...], gil_ref[...], y)
        br, bi = bt[:FFT_N2], bt[FFT_N2:]
        a_ref[p, 0] = br * tr + bi * ti
        a_ref[p, 1] = bi * tr - br * ti
        return carry

    lax.fori_loop(0, n1, stage, 0)
    _lead_inverse_half(a_ref, n1)

    skip = skip_ref[...]
    half = n1 // 2
    for part in range(2):
        y = a_ref[0:half, part].reshape(length, LANES)
        rows = pl.ds(part * length, length)
        o_ref[rows, :] = (x0g_ref[rows, :] * (y + s_ref[rows, :] * skip)).astype(BF16)


def _fftconv_call(s, x0g, khat, skip, fc, length):
    m = s.shape[0]
    npairs = m // (2 * length)
    n1 = fc["n1"]
    nc = HY_W // LANES
    full2 = lambda shape: pl.BlockSpec(shape, lambda j, p: (0, 0))
    full3 = lambda shape: pl.BlockSpec(shape, lambda j, p: (0, 0, 0))
    seq = pl.BlockSpec((2 * length, LANES), lambda j, p: (p, j))
    return pl.pallas_call(
        functools.partial(_fftconv_body, n1=n1, length=length),
        grid=(nc, npairs),
        in_specs=[seq, seq,
                  pl.BlockSpec((1, n1, 2, FFT_N2, LANES), lambda j, p: (j, 0, 0, 0, 0)),
                  pl.BlockSpec((1, LANES), lambda j, p: (0, j)),
                  full2(fc["gfh"].shape), full2(fc["gfl"].shape),
                  full2(fc["gih"].shape), full2(fc["gil"].shape),
                  full3(fc["tar"].shape), full3(fc["tai"].shape),
                  full3(fc["tbr"].shape), full3(fc["tbi"].shape)],
        out_specs=seq,
        out_shape=jax.ShapeDtypeStruct((m, HY_W), BF16),
        scratch_shapes=[pltpu.VMEM((n1, 2, FFT_N2, LANES), F32)],
        compiler_params=_cp(("parallel", "parallel")),
        name="hyena_fftconv",
    )(s, x0g, khat, skip.reshape(1, HY_W), fc["gfh"], fc["gfl"], fc["gih"], fc["gil"],
      fc["tar"], fc["tai"], fc["tbr"], fc["tbi"])


def _head_norm(x, g):
    ms = jnp.mean(x * x, axis=-1, keepdims=True)
    return x * lax.rsqrt(ms + EPS) * g


def _rope(x, cos, sin_signed):
    lane = lax.broadcasted_iota(jnp.int32, x.shape, 1)
    swapped = jnp.where((lane % 64) < 32, pltpu.roll(x, 96, 1), pltpu.roll(x, 32, 1))
    return x * cos + swapped * sin_signed


def _softmax_pv(parts):
    m = None
    for s, _ in parts:
        mi = jnp.max(s, axis=-1, keepdims=True)
        m = mi if m is None else jnp.maximum(m, mi)
    l = None
    o = None
    for s, v in parts:
        p = jnp.exp(s - m)
        li = jnp.sum(p, axis=-1, keepdims=True)
        oi = jnp.dot(p.astype(BF16), v, preferred_element_type=F32)
        l = li if l is None else l + li
        o = oi if o is None else o + oi
    return o / l


def _qk(q, k):
    return lax.dot_general(q, k, (((1,), (1,)), ((), ())), preferred_element_type=F32)


def _gqa_body(q_ref, g_ref, kl_ref, kc_ref, vl_ref, vc_ref, cq_ref, sq_ref, ck_ref, sk_ref,
              qg_ref, kg_ref, o_ref, kn_ref, vb_ref, *, seq, ctx):
    qi = pl.program_id(2)

    @pl.when(qi == 0)
    def _():
        kg = kg_ref[...]
        kn_ref[0:seq, :] = _rope(_head_norm(kl_ref[...], kg), ck_ref[...], sk_ref[...]).astype(BF16)
        kn_ref[seq:seq + ctx, :] = _head_norm(kc_ref[...], kg).astype(BF16)
        vb_ref[0:seq, :] = vl_ref[...].astype(BF16)
        vb_ref[seq:seq + ctx, :] = vc_ref[...].astype(BF16)

    scale = HEAD_DIM ** -0.5
    qg = qg_ref[...]
    cq = cq_ref[...]
    sq = sq_ref[...]
    kn = kn_ref[...]
    vb = vb_ref[...]
    for h in range(ATT_GROUP):
        cols = slice(h * HEAD_DIM, (h + 1) * HEAD_DIM)
        qn = (_rope(_head_norm(q_ref[:, cols], qg), cq, sq) * scale).astype(BF16)
        o = _softmax_pv([(_qk(qn, kn), vb)])
        o_ref[:, cols] = (o * _silu(g_ref[:, cols])).astype(BF16)


def _gqa_call(z, zc, cos_t, sin_t, q_g, k_g, nbatch, seq, ctx):
    tq = min(512, seq)
    nq = seq // tq
    gw = ATT_GROUP * HEAD_DIM
    qspec = lambda off: pl.BlockSpec((tq, gw), lambda b, g, i: (b * nq + i, off // ATT_GROUP + g))
    lat = lambda off: pl.BlockSpec((seq, HEAD_DIM), lambda b, g, i: (b, off + g))
    cxt = lambda off: pl.BlockSpec((ctx, HEAD_DIM), lambda b, g, i: (b, off + g))
    tab_q = pl.BlockSpec((tq, HEAD_DIM), lambda b, g, i: (i, 0))
    tab_k = pl.BlockSpec((seq, HEAD_DIM), lambda b, g, i: (0, 0))
    gain = pl.BlockSpec((1, HEAD_DIM), lambda b, g, i: (0, 0))
    return pl.pallas_call(
        functools.partial(_gqa_body, seq=seq, ctx=ctx),
        grid=(nbatch, ATT_KV_HEADS, nq),
        in_specs=[qspec(EV_Q), qspec(EV_AG), lat(EV_K), cxt(EV_K), lat(EV_VV), cxt(EV_VV),
                  tab_q, tab_q, tab_k, tab_k, gain, gain],
        out_specs=pl.BlockSpec((tq, gw), lambda b, g, i: (b * nq + i, g)),
        out_shape=jax.ShapeDtypeStruct((nbatch * seq, ATT_HEADS * HEAD_DIM), BF16),
        scratch_shapes=[pltpu.VMEM((seq + ctx, HEAD_DIM), BF16), pltpu.VMEM((seq + ctx, HEAD_DIM), BF16)],
        compiler_params=_cp(("parallel", "parallel", "arbitrary")),
        name="gqa_attention",
    )(z, z, z, zc, z, zc, cos_t, sin_t, cos_t, sin_t, q_g.reshape(1, HEAD_DIM), k_g.reshape(1, HEAD_DIM))


def _ctxattn_body(q_ref, g_ref, k_ref, v_ref, qg_ref, kg_ref, o_ref, *, group, norm):
    scale = HEAD_DIM ** -0.5
    k = k_ref[...]
    if norm:
        k = _head_norm(k, kg_ref[...])
    kb = k.astype(BF16)
    vb = v_ref[...].astype(BF16)
    for h in range(group):
        cols = slice(h * HEAD_DIM, (h + 1) * HEAD_DIM)
        q = q_ref[:, cols]
        if norm:
            q = _head_norm(q, qg_ref[...])
        qn = (q * scale).astype(BF16)
        o = _softmax_pv([(_qk(qn, kb), vb)])
        o_ref[:, cols] = (o * _silu(g_ref[:, cols])).astype(BF16)


def _ctxattn_call(zc, q_off, k_off, v_off, g_off, n_kv, group, norm, q_g, k_g, nbatch, ctx):
    gw = group * HEAD_DIM
    qspec = lambda off: pl.BlockSpec((ctx, gw), lambda b, g: (b, off // group + g))
    kspec = lambda off: pl.BlockSpec((ctx, HEAD_DIM), lambda b, g: (b, off + g))
    gain = pl.BlockSpec((1, HEAD_DIM), lambda b, g: (0, 0))
    return pl.pallas_call(
        functools.partial(_ctxattn_body, group=group, norm=norm),
        grid=(nbatch, n_kv),
        in_specs=[qspec(q_off), qspec(g_off), kspec(k_off), kspec(v_off), gain, gain],
        out_specs=pl.BlockSpec((ctx, gw), lambda b, g: (b, g)),
        out_shape=jax.ShapeDtypeStruct((nbatch * ctx, n_kv * gw), BF16),
        compiler_params=_cp(("parallel", "parallel")),
        name="ctx_attention",
    )(zc, zc, zc, zc, q_g.reshape(1, HEAD_DIM), k_g.reshape(1, HEAD_DIM))


def _tile_scan(a, b, rit, reverse):
    total = a.shape[0]
    for sh in (1, 2, 4):
        if not reverse:
            keep = rit >= sh
            a_p = jnp.where(keep, pltpu.roll(a, sh, 0), 1.0)
            b_p = jnp.where(keep, pltpu.roll(b, sh, 0), 0.0)
        else:
            keep = rit < SUBLANES - sh
            a_p = jnp.where(keep, pltpu.roll(a, total - sh, 0), 1.0)
            b_p = jnp.where(keep, pltpu.roll(b, total - sh, 0), 0.0)
        b = a * b_p + b
        a = a * a_p
    return a, b


def _lru_body(ul_ref, uc_ref, gl_ref, gc_ref, cw_ref, cb_ref, wa_ref, ba_ref, wx_ref, bx_ref, lam_ref,
              ol_ref, oc_ref, af_ref, bf_ref, ab_ref, bb_ref, *, seq, ctx):
    total = seq + ctx

    def conv(u_ref, length):
        u = u_ref[...]
        rows = lax.broadcasted_iota(jnp.int32, (length, LANES), 0)
        acc = cb_ref[...] + cw_ref[2:3, :] * u
        acc = acc + cw_ref[0:1, :] * _shift_rows(u, rows, 2, length)
        acc = acc + cw_ref[1:2, :] * _shift_rows(u, rows, 1, length)
        acc = acc + cw_ref[3:4, :] * _shift_rows(u, rows, -1, length)
        return acc

    x_lat = conv(ul_ref, seq)
    x_ctx = conv(uc_ref, ctx)
    layouts = ((af_ref, bf_ref, ((x_ctx, 0), (x_lat, ctx))),
               (ab_ref, bb_ref, ((x_lat, 0), (x_ctx, seq))))
    for d, (a_ref, b_ref, segs) in enumerate(layouts):
        lam = lam_ref[d:d + 1, :]
        neg = -lam
        softplus = jnp.maximum(neg, 0.0) + jnp.log1p(jnp.exp(-jnp.abs(neg)))
        wa = wa_ref[d, 0].astype(BF16)
        wx = wx_ref[d, 0].astype(BF16)
        for x, off in segs:
            xb = x.astype(BF16)
            r = jax.nn.sigmoid(jnp.dot(xb, wa, preferred_element_type=F32) + ba_ref[d:d + 1, :])
            i = jax.nn.sigmoid(jnp.dot(xb, wx, preferred_element_type=F32) + bx_ref[d:d + 1, :])
            log_a = -LRU_C * r * softplus
            a = jnp.exp(log_a)
            bcoef = jnp.sqrt(1.0 - jnp.exp(2.0 * log_a)) * (i * x)
            n = x.shape[0]
            a_ref[off:off + n, :] = a
            b_ref[off:off + n, :] = bcoef

    rit = lax.broadcasted_iota(jnp.int32, (total, LANES), 0) % SUBLANES
    a, b = _tile_scan(af_ref[...], bf_ref[...], rit, False)
    af_ref[...] = a
    bf_ref[...] = b
    a, b = _tile_scan(ab_ref[...], bb_ref[...], rit, True)
    ab_ref[...] = a
    bb_ref[...] = b

    ntiles = total // SUBLANES

    def step(i, carry):
        cf, cb = carry
        rf = pl.ds(pl.multiple_of(i * SUBLANES, SUBLANES), SUBLANES)
        hf = bf_ref[rf, :] + af_ref[rf, :] * cf
        bf_ref[rf, :] = hf
        cf = jnp.broadcast_to(hf[SUBLANES - 1:SUBLANES, :], (SUBLANES, LANES))
        rb = pl.ds(pl.multiple_of((ntiles - 1 - i) * SUBLANES, SUBLANES), SUBLANES)
        hb = bb_ref[rb, :] + ab_ref[rb, :] * cb
        bb_ref[rb, :] = hb
        cb = jnp.broadcast_to(hb[0:1, :], (SUBLANES, LANES))
        return cf, cb

    zero = jnp.zeros((SUBLANES, LANES), F32)
    lax.fori_loop(0, ntiles, step, (zero, zero))

    y_lat = bf_ref[ctx:total, :] + bb_ref[0:seq, :]
    ol_ref[...] = (y_lat * _silu(gl_ref[...])).astype(BF16)
    y_ctx = bf_ref[0:ctx, :] + bb_ref[seq:total, :]
    oc_ref[...] = (y_ctx * _silu(gc_ref[...])).astype(BF16)


def _lru_call(z, zc, conv_w, conv_b, wa, ba, wx, bx, lam, nbatch, seq, ctx):
    nblk = LRU_W // LRU_BS
    lat = lambda off: pl.BlockSpec((seq, LANES), lambda b, j: (b, off + j))
    cxt = lambda off: pl.BlockSpec((ctx, LANES), lambda b, j: (b, off + j))
    vec = lambda rows: pl.BlockSpec((rows, LANES), lambda b, j: (0, j))
    wspec = pl.BlockSpec((2, 1, LRU_BS, LRU_BS), lambda b, j: (0, j, 0, 0))
    total = seq + ctx
    return pl.pallas_call(
        functools.partial(_lru_body, seq=seq, ctx=ctx),
        grid=(nbatch, nblk),
        in_specs=[lat(OD_U), cxt(OD_U), lat(OD_G), cxt(OD_G), vec(4), vec(1),
                  wspec, vec(2), wspec, vec(2), vec(2)],
        out_specs=[pl.BlockSpec((seq, LANES), lambda b, j: (b, j)),
                   pl.BlockSpec((ctx, LANES), lambda b, j: (b, j))],
        out_shape=[jax.ShapeDtypeStruct((nbatch * seq, LRU_W), BF16),
                   jax.ShapeDtypeStruct((nbatch * ctx, LRU_W), BF16)],
        scratch_shapes=[pltpu.VMEM((total, LANES), F32) for _ in range(4)],
        compiler_params=_cp(("parallel", "parallel")),
        name="rglru",
    )(z, zc, z, zc, conv_w, conv_b.reshape(1, LRU_W), wa, ba, wx, bx, lam)


def _na_key_row0(i, rows):
    return jnp.clip(i * NA_TILE_ROWS - NA_WIN_H // 2, 0, rows - NA_KEY_ROWS)


def _na_body(q_ref, g_ref, kl_ref, kc_ref, vl_ref, vc_ref, bias_ref, o_ref, kb_ref, vb_ref, *, seq, ctx):
    i = pl.program_id(2)
    rows = seq // GRID_W

    @pl.when(i == 0)
    def _():
        kb_ref[0:seq, :] = kl_ref[...].astype(BF16)
        kb_ref[seq:seq + ctx, :] = kc_ref[...].astype(BF16)
        vb_ref[0:seq, :] = vl_ref[...].astype(BF16)
        vb_ref[seq:seq + ctx, :] = vc_ref[...].astype(BF16)

    nkeys = NA_KEY_ROWS * GRID_W
    start = pl.multiple_of(_na_key_row0(i, rows) * GRID_W, 256)
    scale = HEAD_DIM ** -0.5
    qn = (q_ref[...] * scale).astype(BF16)
    k_win = kb_ref[pl.ds(start, nkeys), :]
    v_win = vb_ref[pl.ds(start, nkeys), :]
    k_ctx = kb_ref[seq:seq + ctx, :]
    v_ctx = vb_ref[seq:seq + ctx, :]
    s_win = _qk(qn, k_win) + bias_ref[0, 0]
    s_ctx = _qk(qn, k_ctx)
    o = _softmax_pv([(s_win, v_win), (s_ctx, v_ctx)])
    o_ref[...] = (o * _silu(g_ref[...])).astype(BF16)


def _na_call(z, zc, bias, nbatch, seq, ctx):
    rows = seq // GRID_W
    nt = rows // NA_TILE_ROWS
    tq = NA_TILE_ROWS * GRID_W
    nkeys = NA_KEY_ROWS * GRID_W
    qspec = lambda off: pl.BlockSpec((tq, HEAD_DIM), lambda b, h, i: (b * nt + i, off + h))
    lat = lambda off: pl.BlockSpec((seq, HEAD_DIM), lambda b, h, i: (b, off + h))
    cxt = lambda off: pl.BlockSpec((ctx, HEAD_DIM), lambda b, h, i: (b, off + h))

    def btype(i):
        return jnp.where(i == 0, 0, jnp.where(i == nt - 1, 2, 1))

    return pl.pallas_call(
        functools.partial(_na_body, seq=seq, ctx=ctx),
        grid=(nbatch, NA_HEADS, nt),
        in_specs=[qspec(OD_Q), qspec(OD_NG), lat(OD_K), cxt(OD_K), lat(OD_V), cxt(OD_V),
                  pl.BlockSpec((1, 1, tq, nkeys), lambda b, h, i: (h, btype(i), 0, 0))],
        out_specs=pl.BlockSpec((tq, HEAD_DIM), lambda b, h, i: (b * nt + i, h)),
        out_shape=jax.ShapeDtypeStruct((nbatch * seq, NA_HEADS * HEAD_DIM), BF16),
        scratch_shapes=[pltpu.VMEM((seq + ctx, HEAD_DIM), BF16), pltpu.VMEM((seq + ctx, HEAD_DIM), BF16)],
        compiler_params=_cp(("parallel", "parallel", "arbitrary")),
        name="neighbourhood_attention",
    )(z, z, z, zc, z, zc, bias)


def _na_bias(rpb, seq):
    rows = seq // GRID_W
    nt = rows // NA_TILE_ROWS
    qc = np.arange(GRID_W)
    c0 = np.clip(qc - NA_WIN_W // 2, 0, GRID_W - NA_WIN_W)
    kc = np.arange(GRID_W)
    col_ok = (kc[None, :] >= c0[:, None]) & (kc[None, :] < c0[:, None] + NA_WIN_W)
    col_off = np.clip(kc[None, :] - qc[:, None] + NA_WIN_W - 1, 0, 2 * NA_WIN_W - 2)
    cbias = jnp.where(col_ok[None, None], rpb[:, :, col_off], NEG_BIG)
    row_off = np.zeros((3, NA_TILE_ROWS, NA_KEY_ROWS), np.int32)
    row_ok = np.zeros((3, NA_TILE_ROWS, NA_KEY_ROWS), bool)
    for t, i in enumerate((0, min(1, nt - 1), nt - 1)):
        key0 = int(np.clip(i * NA_TILE_ROWS - NA_WIN_H // 2, 0, rows - NA_KEY_ROWS))
        for qr in range(NA_TILE_ROWS):
            r = i * NA_TILE_ROWS + qr
            r0 = int(np.clip(r - NA_WIN_H // 2, 0, rows - NA_WIN_H))
            for kr in range(NA_KEY_ROWS):
                key_row = key0 + kr
                ok = r0 <= key_row < r0 + NA_WIN_H
                row_ok[t, qr, kr] = ok
                row_off[t, qr, kr] = key_row - r + NA_WIN_H - 1 if ok else 0
    full = cbias[:, row_off]
    full = jnp.where(row_ok[None, :, :, :, None, None], full, NEG_BIG)
    full = full.transpose(0, 1, 2, 4, 3, 5)
    return full.reshape(rpb.shape[0], 3, NA_TILE_ROWS * GRID_W, NA_KEY_ROWS * GRID_W)


def _rope_tables(seq):
    pos = jnp.arange(seq, dtype=jnp.int32)
    row = (pos // GRID_W).astype(F32)
    col = (pos % GRID_W).astype(F32)
    n = HEAD_DIM // 4
    inv = ROPE_THETA ** (-jnp.arange(n, dtype=F32) / n)
    ar = row[:, None] * inv
    ac = col[:, None] * inv
    cos_t = jnp.concatenate([jnp.cos(ar), jnp.cos(ar), jnp.cos(ac), jnp.cos(ac)], axis=-1)
    sin_t = jnp.concatenate([-jnp.sin(ar), jnp.sin(ar), -jnp.sin(ac), jnp.sin(ac)], axis=-1)
    return cos_t, sin_t


def _filter_features(length):
    t = jnp.linspace(0.0, 1.0, length, dtype=F32)[:, None]
    bands = (HY_EMB - 1) // 2
    w = 2.0 * math.pi * jnp.arange(length, dtype=F32)[:, None] / length
    fr = jnp.linspace(1e-4, bands - 1, bands, dtype=F32)[None, :]
    feats = jnp.concatenate([t, jnp.cos(fr * w), -jnp.sin(fr * w)], axis=-1)
    r = np.arange(2 * length)
    pos = np.where(r < length, r, (2 * length - r) % length)
    feats2 = feats[pos]
    return jnp.pad(feats2, ((0, 0), (0, LANES - HY_EMB)))


def _pad_to(a, shape):
    return jnp.pad(a, [(0, s - d) for d, s in zip(a.shape, shape)])


def _hyena_out(z, length, filt, conv_w, conv_b, skip, fc):
    kern, asum = _filter_call(*filt, length)
    khat = _kfft_call(kern, asum, fc)
    x0g, s = _hyprep_call(z, conv_w, conv_b, length)
    return _fftconv_call(s, x0g, khat, skip, fc, length)


def kernel(x, c, ctx, c_ctx, w_mod, b_mod, norm_g, final_g, ev_w_in, ev_w_out, hy_conv_w, hy_conv_b, hy_w1, hy_b1, hy_freq, hy_w2, hy_b2, hy_w3, hy_skip, att_q_g, att_k_g, od_w_in, od_w_out, lru_conv_w, lru_conv_b, lru_wa, lru_ba, lru_wx, lru_bx, lru_lambda, na_rpb):
    nbatch, seq, d = x.shape
    nctx = ctx.shape[1]
    depth = w_mod.shape[0]
    assert d == D_MODEL and nbatch % 2 == 0 and nbatch < 8

    c8 = jnp.zeros((8, d), F32).at[:nbatch].set(c).at[nbatch].set(c_ctx)
    mod_all = _mod_call(c8, w_mod, b_mod)

    xl = x.reshape(nbatch * seq, d)
    xc = ctx.reshape(nbatch * nctx, d)

    cos_t, sin_t = _rope_tables(seq)
    deltas = jnp.linspace(abs(math.log(HY_TARGET) / HY_SLOW_PCT), abs(math.log(HY_TARGET) / HY_FAST_PCT),
                          HY_W, dtype=F32).reshape(1, HY_W)
    fc_lat = _fft_consts(seq)
    fc_ctx = _fft_consts(nctx)
    feats_lat = _filter_features(seq)
    feats_ctx = _filter_features(nctx)

    for i in range(depth):
        need_ctx = i < depth - 1
        j = i // 2
        mod3 = mod_all[i].reshape(8, 1, 3 * d)
        h = _normmod_call(xl, mod3, norm_g[i], seq, None)
        hc = _normmod_call(xc, mod3, norm_g[i], nctx, nbatch)
        if i % 2 == 0:
            w_in = ev_w_in[j].astype(BF16)
            w_out = ev_w_out[j].astype(BF16)
            z = _mm_in_call(h, w_in)
            zc = _mm_in_call(hc, w_in)
            filt = lambda feats: (feats, _pad_to(hy_w1[j], (LANES, LANES)), _pad_to(hy_b1[j][None], (1, LANES)),
                                  _pad_to(hy_freq[j][None], (1, LANES)), _pad_to(hy_w2[j], (LANES, LANES)),
                                  _pad_to(hy_b2[j][None], (1, LANES)), _pad_to(hy_w3[j], (LANES, 2 * HY_W)), deltas)
            ya = _hyena_out(z, seq, filt(feats_lat), hy_conv_w[j], hy_conv_b[j], hy_skip[j], fc_lat)
            yb = _gqa_call(z, zc, cos_t, sin_t, att_q_g[j], att_k_g[j], nbatch, seq, nctx)
            xl = _mm_out_call(ya, yb, w_out, xl, mod3, seq, None)
            if need_ctx:
                yac = _hyena_out(zc, nctx, filt(feats_ctx), hy_conv_w[j], hy_conv_b[j], hy_skip[j], fc_ctx)
                ybc = _ctxattn_call(zc, EV_Q, EV_K, EV_VV, EV_AG, ATT_KV_HEADS, ATT_GROUP, True,
                                    att_q_g[j], att_k_g[j], nbatch, nctx)
                xc = _mm_out_call(yac, ybc, w_out, xc, mod3, nctx, nbatch)
        else:
            w_in = od_w_in[j].astype(BF16)
            w_out = od_w_out[j].astype(BF16)
            z = _mm_in_call(h, w_in)
            zc = _mm_in_call(hc, w_in)
            ya, yac = _lru_call(z, zc, lru_conv_w[j], lru_conv_b[j], lru_wa[j], lru_ba[j], lru_wx[j],
                                lru_bx[j], lru_lambda[j], nbatch, seq, nctx)
            yb = _na_call(z, zc, _na_bias(na_rpb[j], seq), nbatch, seq, nctx)
            xl = _mm_out_call(ya, yb, w_out, xl, mod3, seq, None)
            if need_ctx:
                ones = jnp.ones((HEAD_DIM,), F32)
                ybc = _ctxattn_call(zc, OD_Q, OD_K, OD_V, OD_NG, NA_HEADS, 1, False, ones, ones, nbatch, nctx)
                xc = _mm_out_call(yac, ybc, w_out, xc, mod3, nctx, nbatch)

    return _final_call(xl, final_g).reshape(nbatch, seq, d)
```

```python
import functools
import math

import numpy as np
import jax
import jax.numpy as jnp
from jax import lax
from jax.experimental import pallas as pl
from jax.experimental.pallas import tpu as pltpu

F32 = jnp.float32
BF16 = jnp.bfloat16

D_MODEL = 4096
HEAD_DIM = 128
GRID_W = 64
EPS = 1e-6
HY_W = D_MODEL // 2
HY_EMB = 33
HY_FF = 64
HY_TARGET = 1e-2
HY_FAST_PCT = 0.3
HY_SLOW_PCT = 1.5
ATT_HEADS = D_MODEL // 2 // HEAD_DIM
ATT_KV_HEADS = ATT_HEADS // 4
ATT_GROUP = ATT_HEADS // ATT_KV_HEADS
ROPE_THETA = 10000.0
LRU_W = D_MODEL // 2
LRU_BS = 128
LRU_C = 8.0
NA_HEADS = D_MODEL // 2 // HEAD_DIM
NA_WIN_H = 8
NA_WIN_W = 16
NA_TILE_ROWS = 8
NA_KEY_ROWS = 16
NEG_BIG = -1e30
LOG2E = 1.4426950408889634
SCORE_SCALE = HEAD_DIM ** -0.5 * LOG2E

LANES = 128
SUBLANES = 8
FFT_N2 = 128
VMEM_LIMIT_MB = 56

EV_X0, EV_X1, EV_V, EV_HG = 0, 16, 32, 48
EV_Q, EV_K, EV_VV, EV_AG = 64, 80, 84, 88
EVEN_IN = 13312
OD_U, OD_G, OD_Q, OD_K, OD_V, OD_NG = 0, 16, 32, 48, 64, 80
ODD_IN = 12288


def _cp(sem, vmem_mb=VMEM_LIMIT_MB):
    return pltpu.CompilerParams(dimension_semantics=sem, vmem_limit_bytes=vmem_mb * 1024 * 1024)


def _sigmoid(x):
    return 0.5 * jnp.tanh(0.5 * x) + 0.5


def _silu(x):
    return x * _sigmoid(x)


def _mod_body(c_ref, w_ref, b_ref, o_ref):
    sc = _silu(c_ref[...])
    o_ref[0] = jnp.dot(sc.astype(BF16), w_ref[0].astype(BF16), preferred_element_type=F32) + b_ref[0]


def _mod_call(c8, w_mod, b_mod):
    depth, d, n3 = w_mod.shape
    tn = 512
    return pl.pallas_call(
        _mod_body,
        grid=(depth, n3 // tn),
        in_specs=[pl.BlockSpec((8, d), lambda l, j: (0, 0)),
                  pl.BlockSpec((1, d, tn), lambda l, j: (l, 0, j)),
                  pl.BlockSpec((1, 1, tn), lambda l, j: (l, 0, j))],
        out_specs=pl.BlockSpec((1, 8, tn), lambda l, j: (l, 0, j)),
        out_shape=jax.ShapeDtypeStruct((depth, 8, n3), F32),
        compiler_params=_cp(("parallel", "parallel")),
        name="mod_vectors",
    )(c8, w_mod, b_mod.reshape(depth, 1, n3))


def _normmod_body(x_ref, g_ref, sh_ref, sc_ref, o_ref):
    x = x_ref[...]
    ms = jnp.mean(x * x, axis=-1, keepdims=True)
    y = x * lax.rsqrt(ms + EPS) * g_ref[...]
    o_ref[...] = (y * (1.0 + sc_ref[0]) + sh_ref[0]).astype(BF16)


def _mod_row_fn(tr, seq_len, fixed_row):
    if fixed_row is not None:
        return lambda i: fixed_row
    per = seq_len // tr
    return lambda i: i // per


def _normmod_call(x2, mod3, g, seq_len, fixed_row):
    m, d = x2.shape
    tr = 256
    row = _mod_row_fn(tr, seq_len, fixed_row)
    return pl.pallas_call(
        _normmod_body,
        grid=(m // tr,),
        in_specs=[pl.BlockSpec((tr, d), lambda i: (i, 0)),
                  pl.BlockSpec((1, d), lambda i: (0, 0)),
                  pl.BlockSpec((1, 1, d), lambda i: (row(i), 0, 0)),
                  pl.BlockSpec((1, 1, d), lambda i: (row(i), 0, 1))],
        out_specs=pl.BlockSpec((tr, d), lambda i: (i, 0)),
        out_shape=jax.ShapeDtypeStruct((m, d), BF16),
        compiler_params=_cp(("parallel",)),
        name="norm_modulate",
    )(x2, g.reshape(1, d), mod3, mod3)


def _mm_in_body(a_ref, w_ref, o_ref):
    o_ref[...] = jnp.dot(a_ref[...], w_ref[...], preferred_element_type=F32)


def _mm_in_call(a, w, layer):
    m, k = a.shape
    n = w.shape[2]
    tm = min(1024, m)
    tn = 512
    return pl.pallas_call(
        _mm_in_body,
        grid=(m // tm, n // tn),
        in_specs=[pl.BlockSpec((tm, k), lambda i, j: (i, 0)),
                  pl.BlockSpec((None, k, tn), lambda i, j: (layer, 0, j))],
        out_specs=pl.BlockSpec((tm, tn), lambda i, j: (i, j)),
        out_shape=jax.ShapeDtypeStruct((m, n), F32),
        compiler_params=_cp(("parallel", "parallel")),
        name="proj_in",
    )(a, w)


def _mm_out_body(a0_ref, a1_ref, w0_ref, w1_ref, r_ref, gt_ref, o_ref):
    acc = jnp.dot(a0_ref[...], w0_ref[...], preferred_element_type=F32)
    acc = acc + jnp.dot(a1_ref[...], w1_ref[...], preferred_element_type=F32)
    o_ref[...] = r_ref[...] + gt_ref[0] * acc


def _mm_out_call(a0, a1, w, layer, resid, mod3, seq_len, fixed_row):
    m, kh = a0.shape
    n = w.shape[2]
    tm = min(1024, m, seq_len if fixed_row is None else m)
    tn = 512
    row = _mod_row_fn(tm, seq_len, fixed_row)
    gate_blk = 2 * (n // tn)
    return pl.pallas_call(
        _mm_out_body,
        grid=(m // tm, n // tn),
        in_specs=[pl.BlockSpec((tm, kh), lambda i, j: (i, 0)),
                  pl.BlockSpec((tm, kh), lambda i, j: (i, 0)),
                  pl.BlockSpec((None, kh, tn), lambda i, j: (layer, 0, j)),
                  pl.BlockSpec((None, kh, tn), lambda i, j: (layer, 1, j)),
                  pl.BlockSpec((tm, tn), lambda i, j: (i, j)),
                  pl.BlockSpec((1, 1, tn), lambda i, j: (row(i), 0, gate_blk + j))],
        out_specs=pl.BlockSpec((tm, tn), lambda i, j: (i, j)),
        out_shape=jax.ShapeDtypeStruct((m, n), F32),
        compiler_params=_cp(("parallel", "parallel")),
        name="proj_out",
    )(a0, a1, w, w, resid, mod3)


def _final_body(x_ref, g_ref, o_ref):
    x = x_ref[...]
    ms = jnp.mean(x * x, axis=-1, keepdims=True)
    o_ref[...] = x * lax.rsqrt(ms + EPS) * g_ref[...]


def _final_call(x2, g):
    m, d = x2.shape
    tr = 256
    return pl.pallas_call(
        _final_body,
        grid=(m // tr,),
        in_specs=[pl.BlockSpec((tr, d), lambda i: (i, 0)),
                  pl.BlockSpec((1, d), lambda i: (0, 0))],
        out_specs=pl.BlockSpec((tr, d), lambda i: (i, 0)),
        out_shape=jax.ShapeDtypeStruct((m, d), F32),
        compiler_params=_cp(("parallel",)),
        name="final_norm",
    )(x2, g.reshape(1, d))


def _shift_rows(u, rows, k, length):
    if k == 0:
        return u
    if k > 0:
        return jnp.where(rows >= k, pltpu.roll(u, k, 0), 0.0)
    return jnp.where(rows < length + k, pltpu.roll(u, length + k, 0), 0.0)


def _hyprep_body(x0_ref, x1_ref, v_ref, g_ref, w0_ref, w1_ref, w2_ref, b0_ref, b1_ref, b2_ref,
                 x0g_ref, s_ref, *, length):
    rows = lax.broadcasted_iota(jnp.int32, (length, LANES), 0)

    def conv(u_ref, w_ref, b_ref):
        u = u_ref[...]
        return (w_ref[0:1, :] * _shift_rows(u, rows, 1, length) + w_ref[1:2, :] * u
                + w_ref[2:3, :] * _shift_rows(u, rows, -1, length) + b_ref[...])

    x0 = conv(x0_ref, w0_ref, b0_ref)
    x1 = conv(x1_ref, w1_ref, b1_ref)
    v = conv(v_ref, w2_ref, b2_ref)
    x0g_ref[...] = x0 * _silu(g_ref[...])
    s_ref[...] = x1 * v


def _hyprep_call(z, conv_w, conv_b, length):
    m = z.shape[0]
    nseq = m // length
    nc = HY_W // LANES
    cb = conv_b.reshape(1, -1)
    zspec = lambda off: pl.BlockSpec((length, LANES), lambda b, c: (b, off + c))
    wspec = lambda off: pl.BlockSpec((3, LANES), lambda b, c: (0, off + c))
    bspec = lambda off: pl.BlockSpec((1, LANES), lambda b, c: (0, off + c))
    ospec = pl.BlockSpec((length, LANES), lambda b, c: (b, c))
    return pl.pallas_call(
        functools.partial(_hyprep_body, length=length),
        grid=(nseq, nc),
        in_specs=[zspec(EV_X0), zspec(EV_X1), zspec(EV_V), zspec(EV_HG),
                  wspec(EV_X0), wspec(EV_X1), wspec(EV_V),
                  bspec(EV_X0), bspec(EV_X1), bspec(EV_V)],
        out_specs=[ospec, ospec],
        out_shape=[jax.ShapeDtypeStruct((m, HY_W), F32), jax.ShapeDtypeStruct((m, HY_W), F32)],
        compiler_params=_cp(("parallel", "parallel")),
        name="hyena_prep",
    )(z, z, z, z, conv_w, conv_w, conv_w, cb, cb, cb)


def _filter_body(f_ref, w1_ref, b1_ref, fq_ref, w2_ref, b2_ref, w3_ref, dl_ref, k_ref, as_ref,
                 *, length, tr):
    i = pl.program_id(0)
    hp = lax.Precision.HIGHEST
    f = f_ref[...]
    fq = fq_ref[...]
    h1 = jnp.sin(fq * (jnp.dot(f, w1_ref[...], precision=hp, preferred_element_type=F32) + b1_ref[...]))
    h2 = jnp.sin(fq * (jnp.dot(h1, w2_ref[...], precision=hp, preferred_element_type=F32) + b2_ref[...]))
    h = jnp.dot(h2, w3_ref[...], precision=hp, preferred_element_type=F32)
    t = f[:, 0:1]
    dec = jnp.exp(-t * dl_ref[...])
    row = i * tr + lax.broadcasted_iota(jnp.int32, (tr, 1), 0)
    kv = jnp.where(row == length, 0.0, h * dec)
    k_ref[...] = kv

    @pl.when(i == 0)
    def _():
        as_ref[...] = jnp.zeros_like(as_ref)

    as_ref[...] += jnp.sum(jnp.abs(kv), axis=0, keepdims=True)


def _filter_call(feats2, w1p, b1p, fqp, w2p, b2p, w3p, deltas, length):
    n = 2 * length
    tr = min(512, length)
    per_half = length // tr
    full = lambda shape: pl.BlockSpec(shape, lambda i: (0, 0))
    return pl.pallas_call(
        functools.partial(_filter_body, length=length, tr=tr),
        grid=(n // tr,),
        in_specs=[pl.BlockSpec((tr, LANES), lambda i: (i, 0)),
                  full((LANES, LANES)), full((1, LANES)), full((1, LANES)),
                  full((LANES, LANES)), full((1, LANES)),
                  pl.BlockSpec((LANES, HY_W), lambda i: (0, i // per_half)),
                  full((1, HY_W))],
        out_specs=[pl.BlockSpec((tr, HY_W), lambda i: (i, 0)), full((1, HY_W))],
        out_shape=[jax.ShapeDtypeStruct((n, HY_W), F32), jax.ShapeDtypeStruct((1, HY_W), F32)],
        compiler_params=_cp(("arbitrary",)),
        name="hyena_filter",
    )(feats2, w1p, b1p, fqp, w2p, b2p, w3p, deltas)


def _w(n, big_n, sgn):
    n = n % big_n
    if (4 * n) % big_n == 0:
        q = (4 * n) // big_n
        wr, wi = [(1.0, 0.0), (0.0, 1.0), (-1.0, 0.0), (0.0, -1.0)][q]
        return (wr, sgn * wi)
    ang = 2.0 * math.pi * n / big_n
    return (math.cos(ang), sgn * math.sin(ang))


def _smul(a, c):
    if a is None or c == 0.0:
        return None
    if c == 1.0:
        return a
    if c == -1.0:
        return -a
    return a * c


def _sadd(a, b):
    if a is None:
        return b
    if b is None:
        return a
    return a + b


def _ssub(a, b):
    if b is None:
        return a
    if a is None:
        return -b
    return a - b


def _cmulc(x, w):
    xr, xi = x
    wr, wi = w
    return (_ssub(_smul(xr, wr), _smul(xi, wi)), _sadd(_smul(xr, wi), _smul(xi, wr)))


def _dft_list(xs, sgn):
    n = len(xs)
    if n == 1:
        return list(xs)
    ev = _dft_list(xs[0::2], sgn)
    od = _dft_list(xs[1::2], sgn)
    out = [None] * n
    for k in range(n // 2):
        t = _cmulc(od[k], _w(k, n, sgn))
        out[k] = (_sadd(ev[k][0], t[0]), _sadd(ev[k][1], t[1]))
        out[k + n // 2] = (_ssub(ev[k][0], t[0]), _ssub(ev[k][1], t[1]))
    return out


def _split_rc(n1):
    lg = int(round(math.log2(n1)))
    assert (1 << lg) == n1
    r = 1 << (lg // 2)
    return r, n1 // r


def _store_piece(a_ref, slot, r0, val):
    for ri in range(2):
        v = val[ri]
        if v is None:
            v = jnp.zeros((SUBLANES, LANES), F32)
        a_ref[slot, ri, pl.ds(r0, SUBLANES), :] = v


def _load_piece(a_ref, slot, r0):
    return (a_ref[slot, 0, pl.ds(r0, SUBLANES), :], a_ref[slot, 1, pl.ds(r0, SUBLANES), :])


def _lead_forward(load_in, a_ref, n1, in_half):
    r, c = _split_rc(n1)
    nrt = FFT_N2 // SUBLANES

    def pass_a(rt, carry):
        r0 = pl.multiple_of(rt * SUBLANES, SUBLANES)
        for n_c in range(c):
            xs = []
            for n_r in range(r):
                if in_half and n_r >= r // 2:
                    xs.append((None, None))
                else:
                    xs.append(load_in(c * n_r + n_c, r0))
            ys = _dft_list(xs, -1)
            for k_r in range(r):
                _store_piece(a_ref, c * k_r + n_c, r0, _cmulc(ys[k_r], _w(n_c * k_r, n1, -1)))
        return carry

    lax.fori_loop(0, nrt, pass_a, 0)

    def pass_b(rt, carry):
        r0 = pl.multiple_of(rt * SUBLANES, SUBLANES)
        for k_r in range(r):
            xs = [_load_piece(a_ref, c * k_r + n_c, r0) for n_c in range(c)]
            ys = _dft_list(xs, -1)
            for k_c in range(c):
                _store_piece(a_ref, c * k_r + k_c, r0, ys[k_c])
        return carry

    lax.fori_loop(0, nrt, pass_b, 0)


def _lead_inverse_half(a_ref, n1):
    r, c = _split_rc(n1)
    nrt = FFT_N2 // SUBLANES

    def pass_a(rt, carry):
        r0 = pl.multiple_of(rt * SUBLANES, SUBLANES)
        for k_r in range(r):
            xs = [_load_piece(a_ref, c * k_r + k_c, r0) for k_c in range(c)]
            ys = _dft_list(xs, 1)
            for n_c in range(c):
                _store_piece(a_ref, c * k_r + n_c, r0, _cmulc(ys[n_c], _w(n_c * k_r, n1, 1)))
        return carry

    lax.fori_loop(0, nrt, pass_a, 0)

    def pass_b(rt, carry):
        r0 = pl.multiple_of(rt * SUBLANES, SUBLANES)
        for n_c in range(c):
            xs = [_load_piece(a_ref, c * k_r + n_c, r0) for k_r in range(r)]
            ys = _dft_list(xs, 1)
            for n_r in range(r // 2):
                _store_piece(a_ref, c * n_r + n_c, r0, ys[n_r])
        return carry

    lax.fori_loop(0, nrt, pass_b, 0)


def _dot3(g2, x):
    rows = g2.shape[0] // 2
    xh = x.astype(BF16)
    xl = (x - xh.astype(F32)).astype(BF16)
    both = jnp.dot(g2, xh, preferred_element_type=F32)
    return both[:rows] + both[rows:] + jnp.dot(g2[:rows], xl, preferred_element_type=F32)


SLABS_PER_STEP = 2


def _stage_groups(n1):
    return min(2, n1 // SLABS_PER_STEP)


def _load_twiddled(a_ref, p, c, tabs):
    tr, ti = _slab_twiddle(p, c, *tabs)
    ar, ai = a_ref[p, 0], a_ref[p, 1]
    return jnp.concatenate([ar * tr - ai * ti, ar * ti + ai * tr], axis=0), tr, ti


def _slab_twiddle(p, c, tar_ref, tai_ref, tbr_ref, tbi_ref):
    k_r = p // c
    k_c = p % c
    ar, ai = tar_ref[k_r], tai_ref[k_r]
    br, bi = tbr_ref[k_c], tbi_ref[k_c]
    return ar * br - ai * bi, ar * bi + ai * br


def _fft_consts(length):
    n = 2 * length
    n1 = n // FFT_N2
    r, c = _split_rc(n1)
    idx = np.arange(FFT_N2)
    ang = -2.0 * np.pi * np.outer(idx, idx) / FFT_N2
    fr, fi = np.cos(ang), np.sin(ang)
    gf = np.block([[fr, -fi], [fi, fr]])
    gi = np.block([[fr, fi], [-fi, fr]])

    def split(mat):
        m32 = jnp.asarray(mat, F32)
        hi = m32.astype(BF16)
        lo = (m32 - hi.astype(F32)).astype(BF16)
        return jnp.concatenate([hi, lo], axis=0)

    def table(mult, count):
        a = -2.0 * np.pi * np.outer(np.arange(count) * mult, idx) / n
        tr = np.broadcast_to(np.cos(a)[:, :, None], (count, FFT_N2, LANES))
        ti = np.broadcast_to(np.sin(a)[:, :, None], (count, FFT_N2, LANES))
        return jnp.asarray(tr, F32), jnp.asarray(ti, F32)

    tar, tai = table(1, r)
    tbr, tbi = table(r, c)
    return dict(n1=n1, gf2=split(gf), gi2=split(gi), tar=tar, tai=tai, tbr=tbr, tbi=tbi)


def _kfft_body(k_ref, as_ref, gf2_ref, tar_ref, tai_ref, tbr_ref, tbi_ref, o_ref, *, n1):
    a_ref = o_ref.at[0]
    _, c = _split_rc(n1)
    tabs = (tar_ref, tai_ref, tbr_ref, tbi_ref)
    scale = (1.0 / as_ref[...]) * (1.0 / (n1 * FFT_N2))

    def load_in(slab, r0):
        return (k_ref[pl.ds(pl.multiple_of(slab * FFT_N2 + r0, SUBLANES), SUBLANES), :] * scale, None)

    _lead_forward(load_in, a_ref, n1, in_half=False)

    groups = _stage_groups(n1)

    def stage(q, carry):
        pss = [[(q * groups + g) * SLABS_PER_STEP + u for u in range(SLABS_PER_STEP)] for g in range(groups)]
        xs = [jnp.concatenate([_load_twiddled(a_ref, p, c, tabs)[0] for p in ps], axis=1) for ps in pss]
        ys = [_dot3(gf2_ref[...], x) for x in xs]
        for ps, y in zip(pss, ys):
            for u, p in enumerate(ps):
                lanes = slice(u * LANES, (u + 1) * LANES)
                a_ref[p, 0] = y[:FFT_N2, lanes]
                a_ref[p, 1] = y[FFT_N2:, lanes]
        return carry

    lax.fori_loop(0, n1 // (SLABS_PER_STEP * groups), stage, 0)


def _kfft_call(kern, asum, fc):
    n1 = fc["n1"]
    nc = HY_W // LANES
    full2 = lambda shape: pl.BlockSpec(shape, lambda j: (0, 0))
    full3 = lambda shape: pl.BlockSpec(shape, lambda j: (0, 0, 0))
    return pl.pallas_call(
        functools.partial(_kfft_body, n1=n1),
        grid=(nc,),
        in_specs=[pl.BlockSpec((n1 * FFT_N2, LANES), lambda j: (0, j)),
                  pl.BlockSpec((1, LANES), lambda j: (0, j)),
                  full2(fc["gf2"].shape),
                  full3(fc["tar"].shape), full3(fc["tai"].shape),
                  full3(fc["tbr"].shape), full3(fc["tbi"].shape)],
        out_specs=pl.BlockSpec((1, n1, 2, FFT_N2, LANES), lambda j: (j, 0, 0, 0, 0)),
        out_shape=jax.ShapeDtypeStruct((nc, n1, 2, FFT_N2, LANES), F32),
        compiler_params=_cp(("parallel",)),
        name="hyena_filter_fft",
    )(kern, asum, fc["gf2"], fc["tar"], fc["tai"], fc["tbr"], fc["tbi"])


def _fftconv_body(s_ref, x0g_ref, kh_ref, skip_ref, gf2_ref, gi2_ref,
                  tar_ref, tai_ref, tbr_ref, tbi_ref, o_ref, a_ref, *, n1, length):
    _, c = _split_rc(n1)
    tabs = (tar_ref, tai_ref, tbr_ref, tbi_ref)

    def load_in(slab, r0):
        return (s_ref[pl.ds(pl.multiple_of(slab * FFT_N2 + r0, SUBLANES), SUBLANES), :],
                s_ref[pl.ds(pl.multiple_of(length + slab * FFT_N2 + r0, SUBLANES), SUBLANES), :])

    _lead_forward(load_in, a_ref, n1, in_half=True)

    lane_sl = [slice(u * LANES, (u + 1) * LANES) for u in range(SLABS_PER_STEP)]
    groups = _stage_groups(n1)

    def stage(q, carry):
        pss = [[(q * groups + g) * SLABS_PER_STEP + u for u in range(SLABS_PER_STEP)] for g in range(groups)]
        loaded = [[_load_twiddled(a_ref, p, c, tabs) for p in ps] for ps in pss]
        xfs = [_dot3(gf2_ref[...], jnp.concatenate([l[0] for l in ld], axis=1)) for ld in loaded]
        bts = []
        for ps, xf in zip(pss, xfs):
            ys = []
            for p, lanes in zip(ps, lane_sl):
                xr, xi = xf[:FFT_N2, lanes], xf[FFT_N2:, lanes]
                kr, ki = kh_ref[0, p, 0], kh_ref[0, p, 1]
                ys.append(jnp.concatenate([xr * kr - xi * ki, xr * ki + xi * kr], axis=0))
            bts.append(_dot3(gi2_ref[...], jnp.concatenate(ys, axis=1)))
        for ps, ld, bt in zip(pss, loaded, bts):
            for p, lanes, (_, tr, ti) in zip(ps, lane_sl, ld):
                br, bi = bt[:FFT_N2, lanes], bt[FFT_N2:, lanes]
                a_ref[p, 0] = br * tr + bi * ti
                a_ref[p, 1] = bi * tr - br * ti
        return carry

    lax.fori_loop(0, n1 // (SLABS_PER_STEP * groups), stage, 0)
    _lead_inverse_half(a_ref, n1)

    skip = skip_ref[...]
    half = n1 // 2
    for part in range(2):
        y = a_ref[0:half, part].reshape(length, LANES)
        rows = pl.ds(part * length, length)
        o_ref[rows, :] = (x0g_ref[rows, :] * (y + s_ref[rows, :] * skip)).astype(BF16)


def _fftconv_call(s, x0g, khat, skip, fc, length):
    m = s.shape[0]
    npairs = m // (2 * length)
    n1 = fc["n1"]
    nc = HY_W // LANES
    full2 = lambda shape: pl.BlockSpec(shape, lambda j, p: (0, 0))
    full3 = lambda shape: pl.BlockSpec(shape, lambda j, p: (0, 0, 0))
    seq = pl.BlockSpec((2 * length, LANES), lambda j, p: (p, j))
    return pl.pallas_call(
        functools.partial(_fftconv_body, n1=n1, length=length),
        grid=(nc, npairs),
        in_specs=[seq, seq,
                  pl.BlockSpec((1, n1, 2, FFT_N2, LANES), lambda j, p: (j, 0, 0, 0, 0)),
                  pl.BlockSpec((1, LANES), lambda j, p: (0, j)),
                  full2(fc["gf2"].shape), full2(fc["gi2"].shape),
                  full3(fc["tar"].shape), full3(fc["tai"].shape),
                  full3(fc["tbr"].shape), full3(fc["tbi"].shape)],
        out_specs=seq,
        out_shape=jax.ShapeDtypeStruct((m, HY_W), BF16),
        scratch_shapes=[pltpu.VMEM((n1, 2, FFT_N2, LANES), F32)],
        compiler_params=_cp(("parallel", "parallel")),
        name="hyena_fftconv",
    )(s, x0g, khat, skip.reshape(1, HY_W), fc["gf2"], fc["gi2"],
      fc["tar"], fc["tai"], fc["tbr"], fc["tbi"])


def _head_norm(x, g):
    ms = jnp.mean(x * x, axis=-1, keepdims=True)
    return x * lax.rsqrt(ms + EPS) * g


def _rope(x, cos, sin_signed):
    lane = lax.broadcasted_iota(jnp.int32, x.shape, 1)
    swapped = jnp.where((lane % 64) < 32, pltpu.roll(x, 96, 1), pltpu.roll(x, 32, 1))
    return x * cos + swapped * sin_signed


def _softmax_pv(parts):
    m = None
    for s, _ in parts:
        mi = jnp.max(s, axis=-1, keepdims=True)
        m = mi if m is None else jnp.maximum(m, mi)
    l = None
    o = None
    for s, v in parts:
        p = jnp.exp2(s - m)
        li = jnp.sum(p, axis=-1, keepdims=True)
        oi = jnp.dot(p.astype(BF16), v, preferred_element_type=F32)
        l = li if l is None else l + li
        o = oi if o is None else o + oi
    return o / l


def _qk(q, k):
    return lax.dot_general(q, k, (((1,), (1,)), ((), ())), preferred_element_type=F32)


def _gqa_body(q_ref, g_ref, kl_ref, kc_ref, vl_ref, vc_ref, cq_ref, sq_ref, ck_ref, sk_ref,
              qg_ref, kg_ref, o_ref, kn_ref, vb_ref, *, seq, ctx):
    qi = pl.program_id(2)

    @pl.when(qi == 0)
    def _():
        kg = kg_ref[...]
        kn_ref[0:seq, :] = _rope(_head_norm(kl_ref[...], kg), ck_ref[...], sk_ref[...]).astype(BF16)
        kn_ref[seq:seq + ctx, :] = _head_norm(kc_ref[...], kg).astype(BF16)
        vb_ref[0:seq, :] = vl_ref[...].astype(BF16)
        vb_ref[seq:seq + ctx, :] = vc_ref[...].astype(BF16)

    qg = qg_ref[...]
    cq = cq_ref[...]
    sq = sq_ref[...]
    kn = kn_ref[...]
    vb = vb_ref[...]
    heads = [slice(h * HEAD_DIM, (h + 1) * HEAD_DIM) for h in range(ATT_GROUP)]
    qns = [(_rope(_head_norm(q_ref[:, cols], qg), cq, sq) * SCORE_SCALE).astype(BF16) for cols in heads]
    s_next = _qk(qns[0], kn)
    for h, cols in enumerate(heads):
        s = s_next
        if h + 1 < ATT_GROUP:
            s_next = _qk(qns[h + 1], kn)
        o = _softmax_pv([(s, vb)])
        o_ref[:, cols] = (o * _silu(g_ref[:, cols])).astype(BF16)


def _gqa_call(z, zc, cos_t, sin_t, q_g, k_g, nbatch, seq, ctx):
    tq = min(512, seq)
    nq = seq // tq
    gw = ATT_GROUP * HEAD_DIM
    qspec = lambda off: pl.BlockSpec((tq, gw), lambda b, g, i: (b * nq + i, off // ATT_GROUP + g))
    lat = lambda off: pl.BlockSpec((seq, HEAD_DIM), lambda b, g, i: (b, off + g))
    cxt = lambda off: pl.BlockSpec((ctx, HEAD_DIM), lambda b, g, i: (b, off + g))
    tab_q = pl.BlockSpec((tq, HEAD_DIM), lambda b, g, i: (i, 0))
    tab_k = pl.BlockSpec((seq, HEAD_DIM), lambda b, g, i: (0, 0))
    gain = pl.BlockSpec((1, HEAD_DIM), lambda b, g, i: (0, 0))
    return pl.pallas_call(
        functools.partial(_gqa_body, seq=seq, ctx=ctx),
        grid=(nbatch, ATT_KV_HEADS, nq),
        in_specs=[qspec(EV_Q), qspec(EV_AG), lat(EV_K), cxt(EV_K), lat(EV_VV), cxt(EV_VV),
                  tab_q, tab_q, tab_k, tab_k, gain, gain],
        out_specs=pl.BlockSpec((tq, gw), lambda b, g, i: (b * nq + i, g)),
        out_shape=jax.ShapeDtypeStruct((nbatch * seq, ATT_HEADS * HEAD_DIM), BF16),
        scratch_shapes=[pltpu.VMEM((seq + ctx, HEAD_DIM), BF16), pltpu.VMEM((seq + ctx, HEAD_DIM), BF16)],
        compiler_params=_cp(("parallel", "parallel", "arbitrary")),
        name="gqa_attention",
    )(z, z, z, zc, z, zc, cos_t, sin_t, cos_t, sin_t, q_g.reshape(1, HEAD_DIM), k_g.reshape(1, HEAD_DIM))


def _ctxattn_body(q_ref, g_ref, k_ref, v_ref, qg_ref, kg_ref, o_ref, *, group, norm):
    k = k_ref[...]
    if norm:
        k = _head_norm(k, kg_ref[...])
    kb = k.astype(BF16)
    vb = v_ref[...].astype(BF16)
    for h in range(group):
        cols = slice(h * HEAD_DIM, (h + 1) * HEAD_DIM)
        q = q_ref[:, cols]
        if norm:
            q = _head_norm(q, qg_ref[...])
        qn = (q * SCORE_SCALE).astype(BF16)
        o = _softmax_pv([(_qk(qn, kb), vb)])
        o_ref[:, cols] = (o * _silu(g_ref[:, cols])).astype(BF16)


def _ctxattn_call(zc, q_off, k_off, v_off, g_off, n_kv, group, norm, q_g, k_g, nbatch, ctx):
    gw = group * HEAD_DIM
    qspec = lambda off: pl.BlockSpec((ctx, gw), lambda b, g: (b, off // group + g))
    kspec = lambda off: pl.BlockSpec((ctx, HEAD_DIM), lambda b, g: (b, off + g))
    gain = pl.BlockSpec((1, HEAD_DIM), lambda b, g: (0, 0))
    return pl.pallas_call(
        functools.partial(_ctxattn_body, group=group, norm=norm),
        grid=(nbatch, n_kv),
        in_specs=[qspec(q_off), qspec(g_off), kspec(k_off), kspec(v_off), gain, gain],
        out_specs=pl.BlockSpec((ctx, gw), lambda b, g: (b, g)),
        out_shape=jax.ShapeDtypeStruct((nbatch * ctx, n_kv * gw), BF16),
        compiler_params=_cp(("parallel", "parallel")),
        name="ctx_attention",
    )(zc, zc, zc, zc, q_g.reshape(1, HEAD_DIM), k_g.reshape(1, HEAD_DIM))


def _tile_scan(a_ref, b_ref, ntiles, reverse):
    order = range(SUBLANES - 2, -1, -1) if reverse else range(1, SUBLANES)
    for r in order:
        cur = pl.ds(r, ntiles, stride=SUBLANES)
        prv = pl.ds(r + 1 if reverse else r - 1, ntiles, stride=SUBLANES)
        a_c = a_ref[cur, :]
        b_ref[cur, :] = a_c * b_ref[prv, :] + b_ref[cur, :]
        a_ref[cur, :] = a_c * a_ref[prv, :]


def _lru_body(ul_ref, uc_ref, gl_ref, gc_ref, cw_ref, cb_ref, wa_ref, ba_ref, wx_ref, bx_ref, lam_ref,
              ol_ref, oc_ref, af_ref, bf_ref, ab_ref, bb_ref, *, seq, ctx):
    total = seq + ctx

    def conv(u_ref, length):
        u = u_ref[...]
        rows = lax.broadcasted_iota(jnp.int32, (length, LANES), 0)
        acc = cb_ref[...] + cw_ref[2:3, :] * u
        acc = acc + cw_ref[0:1, :] * _shift_rows(u, rows, 2, length)
        acc = acc + cw_ref[1:2, :] * _shift_rows(u, rows, 1, length)
        acc = acc + cw_ref[3:4, :] * _shift_rows(u, rows, -1, length)
        return acc

    x_lat = conv(ul_ref, seq)
    x_ctx = conv(uc_ref, ctx)
    layouts = ((af_ref, bf_ref, ((x_ctx, 0), (x_lat, ctx))),
               (ab_ref, bb_ref, ((x_lat, 0), (x_ctx, seq))))
    for d, (a_ref, b_ref, segs) in enumerate(layouts):
        lam = lam_ref[d:d + 1, :]
        neg = -lam
        softplus = jnp.maximum(neg, 0.0) + jnp.log1p(jnp.exp(-jnp.abs(neg)))
        wa = wa_ref[d, 0].astype(BF16)
        wx = wx_ref[d, 0].astype(BF16)
        for x, off in segs:
            xb = x.astype(BF16)
            r = _sigmoid(jnp.dot(xb, wa, preferred_element_type=F32) + ba_ref[d:d + 1, :])
            i = _sigmoid(jnp.dot(xb, wx, preferred_element_type=F32) + bx_ref[d:d + 1, :])
            a = jnp.exp((-LRU_C * softplus) * r)
            bcoef = jnp.sqrt(1.0 - a * a) * (i * x)
            n = x.shape[0]
            a_ref[off:off + n, :] = a
            b_ref[off:off + n, :] = bcoef

    ntiles = total // SUBLANES
    _tile_scan(af_ref, bf_ref, ntiles, False)
    _tile_scan(ab_ref, bb_ref, ntiles, True)

    def step(i, carry):
        cf, cb = carry
        rf = pl.ds(pl.multiple_of(i * SUBLANES, SUBLANES), SUBLANES)
        hf = bf_ref[rf, :] + af_ref[rf, :] * cf
        bf_ref[rf, :] = hf
        cf = jnp.broadcast_to(hf[SUBLANES - 1:SUBLANES, :], (SUBLANES, LANES))
        rb = pl.ds(pl.multiple_of((ntiles - 1 - i) * SUBLANES, SUBLANES), SUBLANES)
        hb = bb_ref[rb, :] + ab_ref[rb, :] * cb
        bb_ref[rb, :] = hb
        cb = jnp.broadcast_to(hb[0:1, :], (SUBLANES, LANES))
        return cf, cb

    zero = jnp.zeros((SUBLANES, LANES), F32)
    lax.fori_loop(0, ntiles, step, (zero, zero))

    y_lat = bf_ref[ctx:total, :] + bb_ref[0:seq, :]
    ol_ref[...] = (y_lat * _silu(gl_ref[...])).astype(BF16)
    y_ctx = bf_ref[0:ctx, :] + bb_ref[seq:total, :]
    oc_ref[...] = (y_ctx * _silu(gc_ref[...])).astype(BF16)


def _lru_call(z, zc, conv_w, conv_b, wa, ba, wx, bx, lam, nbatch, seq, ctx):
    nblk = LRU_W // LRU_BS
    lat = lambda off: pl.BlockSpec((seq, LANES), lambda b, j: (b, off + j))
    cxt = lambda off: pl.BlockSpec((ctx, LANES), lambda b, j: (b, off + j))
    vec = lambda rows: pl.BlockSpec((rows, LANES), lambda b, j: (0, j))
    wspec = pl.BlockSpec((2, 1, LRU_BS, LRU_BS), lambda b, j: (0, j, 0, 0))
    total = seq + ctx
    return pl.pallas_call(
        functools.partial(_lru_body, seq=seq, ctx=ctx),
        grid=(nbatch, nblk),
        in_specs=[lat(OD_U), cxt(OD_U), lat(OD_G), cxt(OD_G), vec(4), vec(1),
                  wspec, vec(2), wspec, vec(2), vec(2)],
        out_specs=[pl.BlockSpec((seq, LANES), lambda b, j: (b, j)),
                   pl.BlockSpec((ctx, LANES), lambda b, j: (b, j))],
        out_shape=[jax.ShapeDtypeStruct((nbatch * seq, LRU_W), BF16),
                   jax.ShapeDtypeStruct((nbatch * ctx, LRU_W), BF16)],
        scratch_shapes=[pltpu.VMEM((total, LANES), F32) for _ in range(4)],
        compiler_params=_cp(("parallel", "parallel")),
        name="rglru",
    )(z, zc, z, zc, conv_w, conv_b.reshape(1, LRU_W), wa, ba, wx, bx, lam)


def _na_key_row0(i, rows):
    return min(max(i * NA_TILE_ROWS - NA_WIN_H // 2, 0), rows - NA_KEY_ROWS)


def _na_body(q_ref, g_ref, kl_ref, kc_ref, vl_ref, vc_ref, strip_ref, o_ref, kb_ref, vb_ref, bias_ref,
             *, seq, ctx):
    rows = seq // GRID_W
    nt = rows // NA_TILE_ROWS
    tq = NA_TILE_ROWS * GRID_W
    nkeys = NA_KEY_ROWS * GRID_W

    @pl.when(pl.program_id(1) == 0)
    def _():
        for t, i in enumerate((0, 1, nt - 1)):
            bias_ref[t] = _na_tile_bias(strip_ref[0], i, rows)

    kb_ref[0:seq, :] = kl_ref[...].astype(BF16)
    kb_ref[seq:seq + ctx, :] = kc_ref[...].astype(BF16)
    vb_ref[0:seq, :] = vl_ref[...].astype(BF16)
    vb_ref[seq:seq + ctx, :] = vc_ref[...].astype(BF16)
    k_ctx = kb_ref[seq:seq + ctx, :]
    v_ctx = vb_ref[seq:seq + ctx, :]

    def scores(i):
        start = _na_key_row0(i, rows) * GRID_W
        btype = 0 if i == 0 else (2 if i == nt - 1 else 1)
        qn = (q_ref[i * tq:(i + 1) * tq, :] * SCORE_SCALE).astype(BF16)
        s_win = _qk(qn, kb_ref[start:start + nkeys, :]) + bias_ref[btype]
        return s_win, _qk(qn, k_ctx), start

    nxt = scores(0)
    for i in range(nt):
        s_win, s_ctx, start = nxt
        if i + 1 < nt:
            nxt = scores(i + 1)
        o = _softmax_pv([(s_win, vb_ref[start:start + nkeys, :]), (s_ctx, v_ctx)])
        tile = slice(i * tq, (i + 1) * tq)
        o_ref[tile, :] = (o * _silu(g_ref[tile, :])).astype(BF16)


def _na_call(z, zc, strip, nbatch, seq, ctx):
    rows = seq // GRID_W
    nt = rows // NA_TILE_ROWS
    assert nt >= 3
    tq = NA_TILE_ROWS * GRID_W
    nkeys = NA_KEY_ROWS * GRID_W
    lat = lambda off: pl.BlockSpec((seq, HEAD_DIM), lambda h, b: (b, off + h))
    cxt = lambda off: pl.BlockSpec((ctx, HEAD_DIM), lambda h, b: (b, off + h))
    return pl.pallas_call(
        functools.partial(_na_body, seq=seq, ctx=ctx),
        grid=(NA_HEADS, nbatch),
        in_specs=[lat(OD_Q), lat(OD_NG), lat(OD_K), cxt(OD_K), lat(OD_V), cxt(OD_V),
                  pl.BlockSpec((1, GRID_W, nkeys), lambda h, b: (h, 0, 0))],
        out_specs=pl.BlockSpec((seq, HEAD_DIM), lambda h, b: (b, h)),
        out_shape=jax.ShapeDtypeStruct((nbatch * seq, NA_HEADS * HEAD_DIM), BF16),
        scratch_shapes=[pltpu.VMEM((seq + ctx, HEAD_DIM), BF16), pltpu.VMEM((seq + ctx, HEAD_DIM), BF16),
                        pltpu.VMEM((3, tq, nkeys), F32)],
        compiler_params=_cp(("parallel", "arbitrary")),
        name="neighbourhood_attention",
    )(z, z, z, zc, z, zc, strip)


def _na_bias(rpb):
    qc = np.arange(GRID_W)
    c0 = np.clip(qc - NA_WIN_W // 2, 0, GRID_W - NA_WIN_W)
    kc = np.arange(GRID_W)
    col_ok = (kc[None, :] >= c0[:, None]) & (kc[None, :] < c0[:, None] + NA_WIN_W)
    col_off = np.clip(kc[None, :] - qc[:, None] + NA_WIN_W - 1, 0, 2 * NA_WIN_W - 2)
    cbias = jnp.where(col_ok[None, None], rpb[:, :, col_off] * LOG2E, NEG_BIG)
    nheads, ndr = cbias.shape[0], cbias.shape[1]
    strip = cbias.transpose(0, 2, 1, 3).reshape(nheads, GRID_W, ndr * GRID_W)
    return jnp.pad(strip, ((0, 0), (0, 0), (0, NA_KEY_ROWS * GRID_W - ndr * GRID_W)), constant_values=NEG_BIG)


def _na_tile_bias(strip, i, rows):
    nkeys = NA_KEY_ROWS * GRID_W
    key0 = _na_key_row0(i, rows)
    blk = lax.broadcasted_iota(jnp.int32, (GRID_W, nkeys), 1) // GRID_W
    out = []
    for qr in range(NA_TILE_ROWS):
        r = i * NA_TILE_ROWS + qr
        r0 = min(max(r - NA_WIN_H // 2, 0), rows - NA_WIN_H)
        shift = ((r - key0 - (NA_WIN_H - 1)) * GRID_W) % nkeys
        rolled = pltpu.roll(strip, shift, 1) if shift else strip
        lo = r0 - key0
        out.append(jnp.where((blk >= lo) & (blk < lo + NA_WIN_H), rolled, NEG_BIG))
    return jnp.concatenate(out, axis=0)


def _rope_tables(seq):
    pos = jnp.arange(seq, dtype=jnp.int32)
    row = (pos // GRID_W).astype(F32)
    col = (pos % GRID_W).astype(F32)
    n = HEAD_DIM // 4
    inv = ROPE_THETA ** (-jnp.arange(n, dtype=F32) / n)
    ar = row[:, None] * inv
    ac = col[:, None] * inv
    cos_t = jnp.concatenate([jnp.cos(ar), jnp.cos(ar), jnp.cos(ac), jnp.cos(ac)], axis=-1)
    sin_t = jnp.concatenate([-jnp.sin(ar), jnp.sin(ar), -jnp.sin(ac), jnp.sin(ac)], axis=-1)
    return cos_t, sin_t


def _filter_features(length):
    t = jnp.linspace(0.0, 1.0, length, dtype=F32)[:, None]
    bands = (HY_EMB - 1) // 2
    w = 2.0 * math.pi * jnp.arange(length, dtype=F32)[:, None] / length
    fr = jnp.linspace(1e-4, bands - 1, bands, dtype=F32)[None, :]
    feats = jnp.concatenate([t, jnp.cos(fr * w), -jnp.sin(fr * w)], axis=-1)
    r = np.arange(2 * length)
    pos = np.where(r < length, r, (2 * length - r) % length)
    feats2 = feats[pos]
    return jnp.pad(feats2, ((0, 0), (0, LANES - HY_EMB)))


def _pad_to(a, shape):
    return jnp.pad(a, [(0, s - d) for d, s in zip(a.shape, shape)])


def _hyena_out(z, length, filt, conv_w, conv_b, skip, fc):
    kern, asum = _filter_call(*filt, length)
    khat = _kfft_call(kern, asum, fc)
    x0g, s = _hyprep_call(z, conv_w, conv_b, length)
    return _fftconv_call(s, x0g, khat, skip, fc, length)


def kernel(x, c, ctx, c_ctx, w_mod, b_mod, norm_g, final_g, ev_w_in, ev_w_out, hy_conv_w, hy_conv_b, hy_w1, hy_b1, hy_freq, hy_w2, hy_b2, hy_w3, hy_skip, att_q_g, att_k_g, od_w_in, od_w_out, lru_conv_w, lru_conv_b, lru_wa, lru_ba, lru_wx, lru_bx, lru_lambda, na_rpb):
    nbatch, seq, d = x.shape
    nctx = ctx.shape[1]
    depth = w_mod.shape[0]
    assert d == D_MODEL and nbatch % 2 == 0 and nbatch < 8

    c8 = jnp.zeros((8, d), F32).at[:nbatch].set(c).at[nbatch].set(c_ctx)
    mod_all = _mod_call(c8, w_mod, b_mod)

    xl = x.reshape(nbatch * seq, d)
    xc = ctx.reshape(nbatch * nctx, d)

    cos_t, sin_t = _rope_tables(seq)
    deltas = jnp.linspace(abs(math.log(HY_TARGET) / HY_SLOW_PCT), abs(math.log(HY_TARGET) / HY_FAST_PCT),
                          HY_W, dtype=F32).reshape(1, HY_W)
    fc_lat = _fft_consts(seq)
    fc_ctx = _fft_consts(nctx)
    feats_lat = _filter_features(seq)
    feats_ctx = _filter_features(nctx)

    ev_w_in_b, ev_w_out_b = ev_w_in.astype(BF16), ev_w_out.astype(BF16)
    od_w_in_b, od_w_out_b = od_w_in.astype(BF16), od_w_out.astype(BF16)

    for i in range(depth):
        need_ctx = i < depth - 1
        j = i // 2
        mod3 = mod_all[i].reshape(8, 1, 3 * d)
        h = _normmod_call(xl, mod3, norm_g[i], seq, None)
        hc = _normmod_call(xc, mod3, norm_g[i], nctx, nbatch)
        if i % 2 == 0:
            w_in, w_out = ev_w_in_b, ev_w_out_b
            z = _mm_in_call(h, w_in, j)
            zc = _mm_in_call(hc, w_in, j)
            filt = lambda feats: (feats, _pad_to(hy_w1[j], (LANES, LANES)), _pad_to(hy_b1[j][None], (1, LANES)),
                                  _pad_to(hy_freq[j][None], (1, LANES)), _pad_to(hy_w2[j], (LANES, LANES)),
                                  _pad_to(hy_b2[j][None], (1, LANES)), _pad_to(hy_w3[j], (LANES, 2 * HY_W)), deltas)
            ya = _hyena_out(z, seq, filt(feats_lat), hy_conv_w[j], hy_conv_b[j], hy_skip[j], fc_lat)
            yb = _gqa_call(z, zc, cos_t, sin_t, att_q_g[j], att_k_g[j], nbatch, seq, nctx)
            xl = _mm_out_call(ya, yb, w_out, j, xl, mod3, seq, None)
            if need_ctx:
                yac = _hyena_out(zc, nctx, filt(feats_ctx), hy_conv_w[j], hy_conv_b[j], hy_skip[j], fc_ctx)
                ybc = _ctxattn_call(zc, EV_Q, EV_K, EV_VV, EV_AG, ATT_KV_HEADS, ATT_GROUP, True,
                                    att_q_g[j], att_k_g[j], nbatch, nctx)
                xc = _mm_out_call(yac, ybc, w_out, j, xc, mod3, nctx, nbatch)
        else:
            w_in, w_out = od_w_in_b, od_w_out_b
            z = _mm_in_call(h, w_in, j)
            zc = _mm_in_call(hc, w_in, j)
            ya, yac = _lru_call(z, zc, lru_conv_w[j], lru_conv_b[j], lru_wa[j], lru_ba[j], lru_wx[j],
                                lru_bx[j], lru_lambda[j], nbatch, seq, nctx)
            yb = _na_call(z, zc, _na_bias(na_rpb[j]), nbatch, seq, nctx)
            xl = _mm_out_call(ya, yb, w_out, j, xl, mod3, seq, None)
            if need_ctx:
                ones = jnp.ones((HEAD_DIM,), F32)
                ybc = _ctxattn_call(zc, OD_Q, OD_K, OD_V, OD_NG, NA_HEADS, 1, False, ones, ones, nbatch, nctx)
                xc = _mm_out_call(yac, ybc, w_out, j, xc, mod3, nctx, nbatch)

    return _final_call(xl, final_g).reshape(nbatch, seq, d)
```
